```python
import jax, jax.numpy as jnp
from jax import lax
import numpy as np

D_MODEL = 1024
BATCH = 2
SEQ = 16384
DEPTH = 4
DEC_BATCH = 8
DEC_SEQ = 2048
PAST_LEN = 128

CHUNK = 128
A_GROUPS = 8
A_HEAD = 64
D_A = A_GROUPS * A_HEAD
B_GROUPS = 4
B_HEAD = 128
D_B = B_GROUPS * B_HEAD
D_C = 512
CONV_WIDTH = 31
N_BRANCH = 3
IN_WIDTH = 2 * D_A + D_B + 2 * D_C + N_BRANCH * D_MODEL
N_GROUPS = 4
E_PER_GROUP = 8
N_EXPERTS = N_GROUPS * E_PER_GROUP
TOP_K = 2
D_EXPERT = 512
MOE_BLOCK = 256
PLE_DIM = 256
EPS = 1e-6

kernel_name = 'hybrid_gmlp_fnet_conformer_hmoe_encoder'


def rms_norm(x, g):
    xf = x.astype(jnp.float32)
    y = xf * lax.rsqrt(jnp.mean(xf * xf, axis=-1, keepdims=True) + EPS)
    return (y * g.astype(jnp.float32)).astype(x.dtype)


def layer_norm(x, g, b):
    xf = x.astype(jnp.float32)
    mu = jnp.mean(xf, axis=-1, keepdims=True)
    xc = xf - mu
    y = xc * lax.rsqrt(jnp.mean(xc * xc, axis=-1, keepdims=True) + EPS)
    return (y * g.astype(jnp.float32) + b.astype(jnp.float32)).astype(x.dtype)


def spatial_gating(za, ln_g, ln_b, w_s, b_s):
    bt, s = za.shape[0], za.shape[1]
    za = jax.nn.gelu(za, approximate=False)
    u, v = za[..., :D_A], za[..., D_A:]
    v = layer_norm(v, ln_g, ln_b).reshape(bt, s // CHUNK, CHUNK, A_GROUPS, A_HEAD)
    sv = jnp.einsum('gpq,bcqgd->bcpgd', w_s, v) + b_s.T[:, :, None]
    return u * sv.reshape(bt, s, D_A)


def fourier_mix(zb):
    bt, s = zb.shape[0], zb.shape[1]
    zf = zb.reshape(bt, s, B_GROUPS, B_HEAD).astype(jnp.float32)
    y = jnp.fft.fftn(zf, axes=(1, 3), norm='ortho').real
    return y.reshape(bt, s, D_B).astype(zb.dtype)


def conv_module(zc, conv_w, conv_b, ln_g, ln_b):
    a, g = zc[..., :D_C], zc[..., D_C:]
    y = a * jax.nn.sigmoid(g)
    y = lax.conv_general_dilated(y, conv_w[:, None, :], window_strides=(1,), padding='SAME',
                                 dimension_numbers=('NWC', 'WIO', 'NWC'),
                                 feature_group_count=D_C) + conv_b
    return jax.nn.silu(layer_norm(y, ln_g, ln_b))


def hier_moe(h, w_rg, b_rg, w_re, b_re, w1, w3, w2):
    bt, s, d = h.shape
    t = bt * s
    xt = h.reshape(t, d)
    glog = (xt @ w_rg).astype(jnp.float32) + b_rg.astype(jnp.float32)
    gprob = jax.nn.softmax(glog, axis=-1)
    gsel = jnp.argmax(glog, axis=-1)
    pg = jnp.max(gprob, axis=-1)
    elog = ((xt @ w_re).astype(jnp.float32) + b_re.astype(jnp.float32)).reshape(t, N_GROUPS, E_PER_GROUP)
    elog_sel = jnp.take_along_axis(elog, gsel[:, None, None], axis=1)[:, 0]
    top_v, top_i = lax.top_k(elog_sel, TOP_K)
    pe = jax.nn.softmax(top_v, axis=-1)
    eid = gsel[:, None] * E_PER_GROUP + top_i
    wts = pg[:, None] * pe
    a = t * TOP_K
    e_flat = eid.reshape(a)
    w_flat = wts.reshape(a)
    tok = jnp.repeat(jnp.arange(t, dtype=jnp.int32), TOP_K, total_repeat_length=a)
    order = jnp.argsort(e_flat)
    e_s = e_flat[order]
    counts = jnp.bincount(e_flat, length=N_EXPERTS)
    padded = (counts + MOE_BLOCK - 1) // MOE_BLOCK * MOE_BLOCK
    pad_end = jnp.cumsum(padded)
    pad_start = pad_end - padded
    start = jnp.cumsum(counts) - counts
    dest = pad_start[e_s] + jnp.arange(a) - start[e_s]
    n_rows = ((a + MOE_BLOCK - 1) // MOE_BLOCK + N_EXPERTS) * MOE_BLOCK
    n_blocks = n_rows // MOE_BLOCK
    row_tok = jnp.full((n_rows,), t, jnp.int32).at[dest].set(tok[order])
    row_w = jnp.zeros((n_rows,), jnp.float32).at[dest].set(w_flat[order])
    block_e = jnp.minimum(jnp.searchsorted(pad_end, jnp.arange(n_blocks) * MOE_BLOCK, side='right'),
                          N_EXPERTS - 1)
    x_pad = jnp.concatenate([xt, jnp.zeros((1, d), xt.dtype)], axis=0)
    xg = x_pad[row_tok].reshape(n_blocks, MOE_BLOCK, d)

    def expert_block(args):
        xb, e = args
        return (jax.nn.silu(xb @ w1[e]) * (xb @ w3[e])) @ w2[e]

    yg = lax.map(expert_block, (xg, block_e)).reshape(n_rows, d)
    y = jnp.zeros((t + 1, d), h.dtype).at[row_tok].add(yg * row_w[:, None].astype(yg.dtype))
    return y[:t].reshape(bt, s, d)


def _layer(x, pl, norm_mix, w_in, gmlp_ln_g, gmlp_ln_b, w_spatial, b_spatial,
           conv_w, conv_b, conv_ln_g, conv_ln_b, w_out_a, w_out_b, w_out_c, w_o,
           norm_ffn, w_router_group, b_router_group, w_router_expert, b_router_expert,
           w1, w3, w2, norm_ple, w_ple_gate, w_ple_proj):
    bt, s = x.shape[0], x.shape[1]
    h = rms_norm(x, norm_mix)
    z = h @ w_in
    o1 = 2 * D_A
    o2 = o1 + D_B
    o3 = o2 + 2 * D_C
    ya = spatial_gating(z[..., :o1], gmlp_ln_g, gmlp_ln_b, w_spatial, b_spatial)
    yb = fourier_mix(z[..., o1:o2])
    yc = conv_module(z[..., o2:o3], conv_w, conv_b, conv_ln_g, conv_ln_b)
    gates = jax.nn.sigmoid(z[..., o3:]).reshape(bt, s, N_BRANCH, D_MODEL)
    m = (gates[:, :, 0] * (ya @ w_out_a) + gates[:, :, 1] * (yb @ w_out_b)
         + gates[:, :, 2] * (yc @ w_out_c))
    x = x + m @ w_o
    x = x + hier_moe(rms_norm(x, norm_ffn), w_router_group, b_router_group,
                     w_router_expert, b_router_expert, w1, w3, w2)
    g = jax.nn.sigmoid(rms_norm(x, norm_ple) @ w_ple_gate)
    return x + g * (pl @ w_ple_proj)


def setup_inputs(seed: int = 0) -> dict:
    key = jax.random.key(seed)
    ks = jax.random.split(key, 32)

    def nrm(k, shape, scale):
        return jax.random.normal(k, shape, jnp.float32) * scale

    def gain(k, shape):
        return 1.0 + 0.02 * jax.random.normal(k, shape, jnp.float32)

    L = DEPTH
    return {
        'x_prompt': nrm(ks[0], (BATCH, SEQ, D_MODEL), 1.0),
        'x_sample': nrm(ks[1], (DEC_BATCH, DEC_SEQ, D_MODEL), 1.0),
        'p_prompt': nrm(ks[2], (DEPTH, BATCH, SEQ, PLE_DIM), 1.0),
        'p_sample': nrm(ks[3], (DEPTH, DEC_BATCH, DEC_SEQ, PLE_DIM), 1.0),
        'norm_mix': gain(ks[4], (L, D_MODEL)),
        'w_in': nrm(ks[5], (L, D_MODEL, IN_WIDTH), D_MODEL ** -0.5),
        'gmlp_ln_g': gain(ks[6], (L, D_A)),
        'gmlp_ln_b': nrm(ks[7], (L, D_A), 0.02),
        'w_spatial': nrm(ks[8], (L, A_GROUPS, CHUNK, CHUNK), CHUNK ** -0.5),
        'b_spatial': gain(ks[9], (L, A_GROUPS, CHUNK)),
        'conv_w': nrm(ks[10], (L, CONV_WIDTH, D_C), CONV_WIDTH ** -0.5),
        'conv_b': nrm(ks[11], (L, D_C), 0.02),
        'conv_ln_g': gain(ks[12], (L, D_C)),
        'conv_ln_b': nrm(ks[13], (L, D_C), 0.02),
        'w_out_a': nrm(ks[14], (L, D_A, D_MODEL), D_A ** -0.5),
        'w_out_b': nrm(ks[15], (L, D_B, D_MODEL), D_B ** -0.5),
        'w_out_c': nrm(ks[16], (L, D_C, D_MODEL), D_C ** -0.5),
        'w_o': nrm(ks[17], (L, D_MODEL, D_MODEL), D_MODEL ** -0.5),
        'norm_ffn': gain(ks[18], (L, D_MODEL)),
        'w_router_group': nrm(ks[19], (L, D_MODEL, N_GROUPS), D_MODEL ** -0.5),
        'b_router_group': nrm(ks[20], (L, N_GROUPS), 0.01),
        'w_router_expert': nrm(ks[21], (L, D_MODEL, N_EXPERTS), D_MODEL ** -0.5),
        'b_router_expert': nrm(ks[22], (L, N_EXPERTS), 0.01),
        'w1': nrm(ks[23], (L, N_EXPERTS, D_MODEL, D_EXPERT), D_MODEL ** -0.5),
        'w3': nrm(ks[24], (L, N_EXPERTS, D_MODEL, D_EXPERT), D_MODEL ** -0.5),
        'w2': nrm(ks[25], (L, N_EXPERTS, D_EXPERT, D_MODEL), D_EXPERT ** -0.5),
        'norm_ple': gain(ks[26], (L, D_MODEL)),
        'w_ple_gate': nrm(ks[27], (L, D_MODEL, D_MODEL), D_MODEL ** -0.5),
        'w_ple_proj': nrm(ks[28], (L, PLE_DIM, D_MODEL), PLE_DIM ** -0.5),
        'norm_final': gain(ks[29], (D_MODEL,)),
    }


def reference(x_prompt, x_sample, p_prompt, p_sample, norm_mix, w_in, gmlp_ln_g, gmlp_ln_b,
              w_spatial, b_spatial, conv_w, conv_b, conv_ln_g, conv_ln_b, w_out_a, w_out_b,
              w_out_c, w_o, norm_ffn, w_router_group, b_router_group, w_router_expert,
              b_router_expert, w1, w3, w2, norm_ple, w_ple_gate, w_ple_proj, norm_final):
    def run(x, p):
        for i in range(DEPTH):
            x = _layer(x, p[i], norm_mix[i], w_in[i], gmlp_ln_g[i], gmlp_ln_b[i],
                       w_spatial[i], b_spatial[i], conv_w[i], conv_b[i], conv_ln_g[i],
                       conv_ln_b[i], w_out_a[i], w_out_b[i], w_out_c[i], w_o[i], norm_ffn[i],
                       w_router_group[i], b_router_group[i], w_router_expert[i],
                       b_router_expert[i], w1[i], w3[i], w2[i], norm_ple[i], w_ple_gate[i],
                       w_ple_proj[i])
        return rms_norm(x, norm_final)

    y_prompt = run(x_prompt, p_prompt)
    y_sample = run(x_sample, p_sample)
    return (y_prompt, y_sample)
```

```python
import functools
import math

import numpy as np
import jax
import jax.numpy as jnp
from jax import lax
from jax.experimental import pallas as pl
from jax.experimental.pallas import tpu as pltpu

D_MODEL = 1024
CHUNK = 128
A_GROUPS = 8
A_HEAD = 64
D_A = A_GROUPS * A_HEAD
B_GROUPS = 4
B_HEAD = 128
D_B = B_GROUPS * B_HEAD
D_C = 512
CONV_WIDTH = 31
CONV_HALO = 16
N_BRANCH = 3
O1 = 2 * D_A
O2 = O1 + D_B
O3 = O2 + 2 * D_C
IN_WIDTH = O3 + N_BRANCH * D_MODEL
N_GROUPS = 4
E_PER_GROUP = 8
N_EXPERTS = N_GROUPS * E_PER_GROUP
D_EXPERT = 512
MOE_BLOCK = 256
PLE_DIM = 256
EPS = 1e-6

LANES = 128
SUBLANES = 8
FFT_N2 = 128
ROUTE_ROWS = 128
EXPERT_ROW0 = 8
VMEM_LIMIT = 56 * 1024 * 1024

F32 = jnp.float32
BF16 = jnp.bfloat16


def _cparams(*sem):
    return pltpu.CompilerParams(dimension_semantics=sem, vmem_limit_bytes=VMEM_LIMIT)


def _resident(shape, index=None):
    idx = tuple(index) if index is not None else (0,) * len(shape)
    return pl.BlockSpec(shape, lambda *_: idx, pipeline_mode=pl.Buffered(1))


def _rms(x, g):
    return x * lax.rsqrt(jnp.mean(x * x, axis=-1, keepdims=True) + EPS) * g


def _ln(x, g, b):
    mu = jnp.mean(x, axis=-1, keepdims=True)
    xc = x - mu
    return xc * lax.rsqrt(jnp.mean(xc * xc, axis=-1, keepdims=True) + EPS) * g + b


def _sigmoid(x):
    return 1.0 / (1.0 + jnp.exp(-x))


def _dot(a, b):
    return jnp.dot(a, b, preferred_element_type=F32)


def _store_rows(ref, val):
    for j in range(SUBLANES):
        ref[:, j, :] = val[:, j * LANES:(j + 1) * LANES]


def _mix_in_kernel(x_ref, nm_ref, w_ref, lng_ref, lnb_ref, wsp_ref, bsp_ref, dft_ref,
                   ya_ref, xr_ref, xi_ref, yglu_ref, gates_ref, *, tm):
    hb = _rms(x_ref[...], nm_ref[...]).astype(BF16)

    def proj(lo, hi):
        return _dot(hb, w_ref[:, lo:hi])

    za = proj(0, O1)
    za = 0.5 * za * (1.0 + lax.erf(za * (1.0 / math.sqrt(2.0))))
    u = za[:, :D_A]
    v = _ln(za[:, D_A:], lng_ref[...], lnb_ref[...])
    n_chunks = tm // CHUNK
    low_half = lax.broadcasted_iota(jnp.int32, (CHUNK, LANES), 1) < A_HEAD
    for j in range(A_GROUPS // 2):
        cols = slice(j * LANES, (j + 1) * LANES)
        pieces = []
        for c in range(n_chunks):
            vp = v[c * CHUNK:(c + 1) * CHUNK, cols]
            pieces.append(jnp.concatenate([jnp.where(low_half, vp, 0.0), jnp.where(low_half, 0.0, vp)], axis=0))
        rhs = jnp.concatenate(pieces, axis=1).astype(BF16)
        sv = _dot(wsp_ref[j], rhs)
        for c in range(n_chunks):
            rows = slice(c * CHUNK, (c + 1) * CHUNK)
            svc = sv[:, c * LANES:(c + 1) * LANES] + bsp_ref[j]
            ya_ref[rows, cols] = (u[rows, cols] * svc).astype(ya_ref.dtype)

    zb = proj(O1, O2).astype(BF16)
    for g in range(B_GROUPS):
        cols = slice(g * B_HEAD, (g + 1) * B_HEAD)
        c = _dot(zb[:, cols], dft_ref[...])
        xr_ref[:, cols] = c[:, :B_HEAD]
        xi_ref[:, cols] = c[:, B_HEAD:]

    zc = proj(O2, O3)
    yglu_ref[...] = zc[:, :D_C] * _sigmoid(zc[:, D_C:])

    for b in range(N_BRANCH):
        cols = slice(b * D_MODEL, (b + 1) * D_MODEL)
        gates_ref[:, cols] = _sigmoid(proj(O3 + b * D_MODEL, O3 + (b + 1) * D_MODEL)).astype(gates_ref.dtype)


def _mix_in(layer, x, nm, w_in, lng, lnb, wsp, bsp, dftc, tm):
    t = x.shape[0]

    def row(w):
        return pl.BlockSpec((tm, w), lambda i: (i, 0))

    def lay(shape):
        return pl.BlockSpec((None,) + shape, lambda i: (layer,) + (0,) * len(shape), pipeline_mode=pl.Buffered(1))

    return pl.pallas_call(
        functools.partial(_mix_in_kernel, tm=tm),
        grid=(t // tm,),
        in_specs=[row(D_MODEL), lay((1, D_MODEL)), lay((D_MODEL, IN_WIDTH)), lay((1, D_A)), lay((1, D_A)),
                  lay((A_GROUPS // 2, CHUNK, 2 * CHUNK)), lay((A_GROUPS // 2, CHUNK, LANES)),
                  _resident((B_HEAD, 2 * B_HEAD))],
        out_specs=[row(D_A), row(D_B), row(D_B), row(D_C), row(N_BRANCH * D_MODEL)],
        out_shape=[jax.ShapeDtypeStruct((t, D_A), BF16), jax.ShapeDtypeStruct((t, D_B), F32),
                   jax.ShapeDtypeStruct((t, D_B), F32), jax.ShapeDtypeStruct((t, D_C), F32),
                   jax.ShapeDtypeStruct((t, N_BRANCH * D_MODEL), BF16)],
        compiler_params=_cparams("parallel"),
        name="mix_in",
    )(x, nm, w_in, lng, lnb, wsp, bsp, dftc)


def _fft1_kernel(xr_ref, xi_ref, a_ref, twr_ref, twi_ref, y_ref, *, n1):
    for j in range(SUBLANES):
        x = jnp.concatenate([xr_ref[:, j, :], xi_ref[:, j, :]], axis=0).astype(BF16)
        y = _dot(a_ref[...], x)
        yr, yi = y[:n1], y[n1:]
        tr, ti = twr_ref[j], twi_ref[j]
        y_ref[0, j] = (yr * tr - yi * ti).astype(y_ref.dtype)
        y_ref[1, j] = (yr * ti + yi * tr).astype(y_ref.dtype)


def _fft1(xr, xi, a_mat, twr, twi, bsz, n1):
    n2 = FFT_N2
    xr4 = xr.reshape(bsz, n1, n2, D_B)
    xi4 = xi.reshape(bsz, n1, n2, D_B)
    blk = pl.BlockSpec((None, n1, SUBLANES, D_B), lambda b, j: (b, 0, j, 0))
    tw = pl.BlockSpec((SUBLANES, n1, 1), lambda b, j: (j, 0, 0))
    return pl.pallas_call(
        functools.partial(_fft1_kernel, n1=n1),
        grid=(bsz, n2 // SUBLANES),
        in_specs=[blk, blk, _resident((2 * n1, 2 * n1)), tw, tw],
        out_specs=pl.BlockSpec((None, 2, SUBLANES, n1, D_B), lambda b, j: (b, 0, j, 0, 0)),
        out_shape=jax.ShapeDtypeStruct((bsz, 2, n2, n1, D_B), BF16),
        compiler_params=_cparams("parallel", "parallel"),
        name="fft1",
    )(xr4, xi4, a_mat, twr, twi)


def _fft2_kernel(y_ref, w_ref, o_ref):
    y = jnp.concatenate([y_ref[0], y_ref[1]], axis=0)
    o_ref[...] = _dot(w_ref[...], y).astype(o_ref.dtype)


def _fft2(ycat, w2, bsz, n1, cols_blk):
    n2 = FFT_N2
    width = n1 * D_B
    y4 = ycat.reshape(bsz, 2, n2, width)
    return pl.pallas_call(
        _fft2_kernel,
        grid=(bsz, width // cols_blk),
        in_specs=[pl.BlockSpec((None, 2, n2, cols_blk), lambda b, c: (b, 0, 0, c)), _resident((n2, 2 * n2))],
        out_specs=pl.BlockSpec((None, n2, cols_blk), lambda b, c: (b, 0, c)),
        out_shape=jax.ShapeDtypeStruct((bsz, n2, width), BF16),
        compiler_params=_cparams("parallel", "parallel"),
        name="fft2",
    )(y4, w2)


def _dft_constants(n1):
    n2 = FFT_N2
    k = np.arange(B_HEAD)
    ang = 2.0 * np.pi * np.outer(k, k) / B_HEAD
    dftc = np.concatenate([np.cos(ang), -np.sin(ang)], axis=1) / math.sqrt(B_HEAD)
    k1 = np.arange(n1)
    ang1 = 2.0 * np.pi * np.outer(k1, k1) / n1
    ar, ai = np.cos(ang1) / math.sqrt(n1), -np.sin(ang1) / math.sqrt(n1)
    a_mat = np.block([[ar, -ai], [ai, ar]])
    n2i = np.arange(n2)
    angt = 2.0 * np.pi * np.outer(n2i, k1) / (n1 * n2)
    twr, twi = np.cos(angt)[:, :, None], -np.sin(angt)[:, :, None]
    ang2 = 2.0 * np.pi * np.outer(n2i, n2i) / n2
    w2 = np.concatenate([np.cos(ang2), np.sin(ang2)], axis=1) / math.sqrt(n2)
    return (jnp.asarray(dftc, BF16), jnp.asarray(a_mat, BF16), jnp.asarray(twr, F32), jnp.asarray(twi, F32),
            jnp.asarray(w2, BF16))


CONV_ROWS = 32


def _conv_kernel(cur_ref, prev_ref, next_ref, w_ref, b_ref, g_ref, bb_ref, o_ref, ext_ref, *, tc, seq):
    i = pl.program_id(0)
    first = (i * tc) % seq == 0
    last = ((i + 1) * tc) % seq == 0
    ext_ref[0:CONV_HALO] = jnp.where(first, 0.0, prev_ref[...])
    ext_ref[CONV_HALO:CONV_HALO + tc] = cur_ref[...]
    ext_ref[CONV_HALO + tc:2 * CONV_HALO + tc] = jnp.where(last, 0.0, next_ref[...])
    base = CONV_HALO - CONV_WIDTH // 2
    for r in range(tc // CONV_ROWS):
        acc = jnp.zeros((CONV_ROWS, D_C), F32)
        for k in range(CONV_WIDTH):
            lo = r * CONV_ROWS + base + k
            acc = acc + w_ref[k:k + 1, :] * ext_ref[lo:lo + CONV_ROWS, :]
        y = _ln(acc + b_ref[...], g_ref[...], bb_ref[...])
        o_ref[r * CONV_ROWS:(r + 1) * CONV_ROWS] = (y * _sigmoid(y)).astype(o_ref.dtype)


def _conv(layer, yglu, conv_w, conv_b, lng, lnb, seq, tc):
    t = yglu.shape[0]
    hb = tc // CONV_HALO
    n_halo = t // CONV_HALO

    def lay(shape):
        return pl.BlockSpec((None,) + shape, lambda i: (layer,) + (0,) * len(shape), pipeline_mode=pl.Buffered(1))

    return pl.pallas_call(
        functools.partial(_conv_kernel, tc=tc, seq=seq),
        grid=(t // tc,),
        in_specs=[pl.BlockSpec((tc, D_C), lambda i: (i, 0)),
                  pl.BlockSpec((CONV_HALO, D_C), lambda i: (jnp.maximum(i * hb - 1, 0), 0)),
                  pl.BlockSpec((CONV_HALO, D_C), lambda i: (jnp.minimum((i + 1) * hb, n_halo - 1), 0)),
                  lay((CONV_WIDTH + 1, D_C)), lay((1, D_C)), lay((1, D_C)), lay((1, D_C))],
        out_specs=pl.BlockSpec((tc, D_C), lambda i: (i, 0)),
        out_shape=jax.ShapeDtypeStruct((t, D_C), BF16),
        scratch_shapes=[pltpu.VMEM((tc + 2 * CONV_HALO, D_C), F32)],
        compiler_params=_cparams("parallel"),
        name="conv",
    )(yglu, yglu, yglu, conv_w, conv_b, lng, lnb)


def _mix_out_kernel(x_ref, ya_ref, yb_ref, yc_ref, gt_ref, wa_ref, wb_ref, wc_ref, wo_ref, nf_ref,
                    wr_ref, br_ref, x1_ref, hn_ref, ri_ref, rw_ref, *, tm):
    m = gt_ref[:, 0:D_MODEL].astype(F32) * _dot(ya_ref[...], wa_ref[...])
    m = m + gt_ref[:, D_MODEL:2 * D_MODEL].astype(F32) * _dot(yb_ref[...], wb_ref[...])
    m = m + gt_ref[:, 2 * D_MODEL:3 * D_MODEL].astype(F32) * _dot(yc_ref[...], wc_ref[...])
    x1 = x_ref[...] + _dot(m.astype(BF16), wo_ref[...])
    x1_ref[...] = x1
    hn = _rms(x1, nf_ref[...])
    _store_rows(hn_ref, hn)

    lg = lax.dot_general(wr_ref[...], hn, (((1,), (1,)), ((), ())), precision=lax.Precision.HIGHEST,
                         preferred_element_type=F32) + br_ref[...]
    gl = [lg[g:g + 1, :] for g in range(N_GROUPS)]
    best = gl[0]
    gsel = jnp.zeros((1, tm), jnp.int32)
    for g in range(1, N_GROUPS):
        better = gl[g] > best
        best = jnp.where(better, gl[g], best)
        gsel = jnp.where(better, g, gsel)
    denom = jnp.exp(gl[0] - best)
    for g in range(1, N_GROUPS):
        denom = denom + jnp.exp(gl[g] - best)
    pg = 1.0 / denom
    es = lg[EXPERT_ROW0:EXPERT_ROW0 + E_PER_GROUP, :]
    for g in range(1, N_GROUPS):
        lo = EXPERT_ROW0 + g * E_PER_GROUP
        es = jnp.where(gsel == g, lg[lo:lo + E_PER_GROUP, :], es)
    slot = lax.broadcasted_iota(jnp.int32, (E_PER_GROUP, tm), 0)
    v0 = jnp.max(es, axis=0, keepdims=True)
    i0 = jnp.min(jnp.where(es == v0, slot, E_PER_GROUP), axis=0, keepdims=True)
    es1 = jnp.where(slot == i0, -jnp.inf, es)
    v1 = jnp.max(es1, axis=0, keepdims=True)
    i1 = jnp.min(jnp.where(es1 == v1, slot, E_PER_GROUP), axis=0, keepdims=True)
    ex = jnp.exp(v1 - v0)
    p0 = 1.0 / (1.0 + ex)
    p1 = ex / (1.0 + ex)
    row = lax.broadcasted_iota(jnp.int32, (SUBLANES, tm), 0)
    e0 = gsel * E_PER_GROUP + i0
    e1 = gsel * E_PER_GROUP + i1
    ri_ref[...] = jnp.where(row == 0, e0, jnp.where(row == 1, e1, 0))
    rw_ref[...] = jnp.where(row == 0, pg * p0, jnp.where(row == 1, pg * p1, 0.0))


def _mix_out(layer, x, ya, yb, yc, gates, wa, wb, wc, wo, nf, wr, br, tm):
    t = x.shape[0]

    def row(w):
        return pl.BlockSpec((tm, w), lambda i: (i, 0))

    def lay(shape):
        return pl.BlockSpec((None,) + shape, lambda i: (layer,) + (0,) * len(shape), pipeline_mode=pl.Buffered(1))

    col = pl.BlockSpec((SUBLANES, tm), lambda i: (0, i))
    return pl.pallas_call(
        functools.partial(_mix_out_kernel, tm=tm),
        grid=(t // tm,),
        in_specs=[row(D_MODEL), row(D_A), row(D_B), row(D_C), row(N_BRANCH * D_MODEL),
                  lay((D_A, D_MODEL)), lay((D_B, D_MODEL)), lay((D_C, D_MODEL)), lay((D_MODEL, D_MODEL)),
                  lay((1, D_MODEL)), lay((ROUTE_ROWS, D_MODEL)), lay((ROUTE_ROWS, 1))],
        out_specs=[row(D_MODEL), pl.BlockSpec((tm, SUBLANES, LANES), lambda i: (i, 0, 0)), col, col],
        out_shape=[jax.ShapeDtypeStruct((t, D_MODEL), F32), jax.ShapeDtypeStruct((t, SUBLANES, LANES), F32),
                   jax.ShapeDtypeStruct((SUBLANES, t), jnp.int32), jax.ShapeDtypeStruct((SUBLANES, t), F32)],
        compiler_params=_cparams("parallel"),
        name="mix_out",
    )(x, ya, yb, yc, gates, wa, wb, wc, wo, nf, wr, br)


def _route_plan(ri, rw, t):
    a = 2 * t
    e_flat = ri[:2].reshape(a)
    w_flat = rw[:2].reshape(a)
    onehot = (e_flat[:, None] == jnp.arange(N_EXPERTS, dtype=jnp.int32)[None, :]).astype(jnp.int32)
    csum = jnp.cumsum(onehot, axis=0)
    counts = csum[-1]
    padded = (counts + MOE_BLOCK - 1) // MOE_BLOCK * MOE_BLOCK
    pad_end = jnp.cumsum(padded)
    pad_start = pad_end - padded
    dest = jnp.sum(onehot * (csum - 1 + pad_start[None, :]), axis=1)
    n_blocks = a // MOE_BLOCK + N_EXPERTS
    n_rows = n_blocks * MOE_BLOCK
    tok = jnp.tile(jnp.arange(t, dtype=jnp.int32), 2)
    row_tok = jnp.zeros((n_rows,), jnp.int32).at[dest].set(tok)
    row_w = jnp.zeros((n_rows,), F32).at[dest].set(w_flat)
    block_e = jnp.minimum(jnp.searchsorted(pad_end, jnp.arange(n_blocks, dtype=jnp.int32) * MOE_BLOCK, side='right'),
                          N_EXPERTS - 1).astype(jnp.int32)
    n_used = (pad_end[-1] // MOE_BLOCK).astype(jnp.int32).reshape(1)
    return dest.astype(jnp.int32), row_tok, row_w, block_e, n_used, n_blocks


def _gathered_rows(buf, slot, lo, n):
    return jnp.concatenate([buf[slot, lo:lo + n, j, :] for j in range(SUBLANES)], axis=1)


def _expert_kernel(be_ref, nu_ref, tok_ref, tokn_ref, hn_hbm, rw_ref, w1_ref, w3_ref, w2_ref, o_ref, xbuf, sem):
    i = pl.program_id(0)
    n_used = nu_ref[0]
    slot = i % 2

    def issue(tref, s):
        def body(r, carry):
            pltpu.make_async_copy(hn_hbm.at[tref[0, 0, r]], xbuf.at[s, r], sem.at[s]).start()
            return carry
        lax.fori_loop(0, MOE_BLOCK, body, 0)

    @pl.when(i == 0)
    def _():
        issue(tok_ref, 0)

    @pl.when(i + 1 < n_used)
    def _():
        issue(tokn_ref, 1 - slot)

    @pl.when(i < n_used)
    def _():
        pltpu.make_async_copy(hn_hbm.at[pl.ds(0, MOE_BLOCK)], xbuf.at[slot], sem.at[slot]).wait()
        xb = _gathered_rows(xbuf, slot, 0, MOE_BLOCK).astype(BF16)
        h1 = _dot(xb, w1_ref[...])
        h3 = _dot(xb, w3_ref[...])
        act = (h1 * _sigmoid(h1) * h3).astype(BF16)
        _store_rows(o_ref, _dot(act, w2_ref[...]) * rw_ref[...])

    @pl.when(i >= n_used)
    def _():
        o_ref[...] = jnp.zeros_like(o_ref)


def _experts(layer, hn3, row_tok, row_w, block_e, n_used, n_blocks, w1, w3, w2):
    tok3 = row_tok.reshape(n_blocks, 1, MOE_BLOCK)
    grid_spec = pltpu.PrefetchScalarGridSpec(
        num_scalar_prefetch=2,
        grid=(n_blocks,),
        in_specs=[
            pl.BlockSpec((1, 1, MOE_BLOCK), lambda i, be, nu: (i, 0, 0), memory_space=pltpu.SMEM),
            pl.BlockSpec((1, 1, MOE_BLOCK), lambda i, be, nu: (jnp.minimum(i + 1, n_blocks - 1), 0, 0),
                         memory_space=pltpu.SMEM),
            pl.BlockSpec(memory_space=pl.ANY),
            pl.BlockSpec((MOE_BLOCK, 1), lambda i, be, nu: (i, 0)),
            pl.BlockSpec((None, None, D_MODEL, D_EXPERT), lambda i, be, nu: (layer, be[i], 0, 0)),
            pl.BlockSpec((None, None, D_MODEL, D_EXPERT), lambda i, be, nu: (layer, be[i], 0, 0)),
            pl.BlockSpec((None, None, D_EXPERT, D_MODEL), lambda i, be, nu: (layer, be[i], 0, 0)),
        ],
        out_specs=pl.BlockSpec((MOE_BLOCK, SUBLANES, LANES), lambda i, be, nu: (i, 0, 0)),
        scratch_shapes=[pltpu.VMEM((2, MOE_BLOCK, SUBLANES, LANES), F32), pltpu.SemaphoreType.DMA((2,))],
    )
    return pl.pallas_call(
        _expert_kernel,
        grid_spec=grid_spec,
        out_shape=jax.ShapeDtypeStruct((n_blocks * MOE_BLOCK, SUBLANES, LANES), F32),
        compiler_params=_cparams("arbitrary"),
        name="experts",
    )(block_e, n_used, tok3, tok3, hn3, row_w.reshape(-1, 1), w1, w3, w2)


def _combine_kernel(d_ref, dn_ref, yg_hbm, x1_ref, p_ref, np_ref, wg_ref, wp_ref, nfin_ref, o_ref, gbuf, sem,
                    *, tq, n_tiles, final):
    i = pl.program_id(0)
    slot = i % 2

    def issue(dref, s):
        def body(r, carry):
            pltpu.make_async_copy(yg_hbm.at[dref[0, 0, r]], gbuf.at[s, r], sem.at[s]).start()
            return carry
        lax.fori_loop(0, 2 * tq, body, 0)

    @pl.when(i == 0)
    def _():
        issue(d_ref, 0)

    @pl.when(i + 1 < n_tiles)
    def _():
        issue(dn_ref, 1 - slot)

    pltpu.make_async_copy(yg_hbm.at[pl.ds(0, 2 * tq)], gbuf.at[slot], sem.at[slot]).wait()
    moe = _gathered_rows(gbuf, slot, 0, tq) + _gathered_rows(gbuf, slot, tq, tq)
    x2 = x1_ref[...] + moe
    g = _sigmoid(_dot(_rms(x2, np_ref[...]).astype(BF16), wg_ref[...]))
    x3 = x2 + g * _dot(p_ref[...].astype(BF16), wp_ref[...])
    if final:
        x3 = _rms(x3, nfin_ref[...])
    o_ref[...] = x3


def _combine(layer, dest, yg3, x1, p, norm_ple, wg, wp, nfin, tq, final):
    t = x1.shape[0]
    n_tiles = t // tq
    d3 = dest.reshape(2, n_tiles, tq).transpose(1, 0, 2).reshape(n_tiles, 1, 2 * tq)

    def lay(shape):
        return pl.BlockSpec((None,) + shape, lambda i: (layer,) + (0,) * len(shape), pipeline_mode=pl.Buffered(1))

    return pl.pallas_call(
        functools.partial(_combine_kernel, tq=tq, n_tiles=n_tiles, final=final),
        grid=(n_tiles,),
        in_specs=[
            pl.BlockSpec((1, 1, 2 * tq), lambda i: (i, 0, 0), memory_space=pltpu.SMEM),
            pl.BlockSpec((1, 1, 2 * tq), lambda i: (jnp.minimum(i + 1, n_tiles - 1), 0, 0), memory_space=pltpu.SMEM),
            pl.BlockSpec(memory_space=pl.ANY),
            pl.BlockSpec((tq, D_MODEL), lambda i: (i, 0)),
            pl.BlockSpec((None, tq, PLE_DIM), lambda i: (layer, i, 0)),
            lay((1, D_MODEL)), lay((D_MODEL, D_MODEL)), lay((PLE_DIM, D_MODEL)), _resident((1, D_MODEL)),
        ],
        out_specs=pl.BlockSpec((tq, D_MODEL), lambda i: (i, 0)),
        out_shape=jax.ShapeDtypeStruct((t, D_MODEL), F32),
        scratch_shapes=[pltpu.VMEM((2, 2 * tq, SUBLANES, LANES), F32), pltpu.SemaphoreType.DMA((2,))],
        compiler_params=_cparams("arbitrary"),
        name="combine",
    )(d3, d3, yg3, x1, p, norm_ple, wg, wp, nfin)


def _tile(t, want):
    return want if t % want == 0 else CHUNK


def _run_group(x, p, wts):
    bsz, seq, _ = x.shape
    depth = p.shape[0]
    t = bsz * seq
    n1 = seq // FFT_N2
    tm = _tile(t, 512)
    tq = _tile(t, 256)
    dftc, a_mat, twr, twi, w2f = _dft_constants(n1)
    xf = x.reshape(t, D_MODEL)
    pf = p.reshape(depth, t, PLE_DIM)
    for l in range(depth):
        ya, xr, xi, yglu, gates = _mix_in(l, xf, wts['norm_mix'], wts['w_in'], wts['gmlp_ln_g'], wts['gmlp_ln_b'],
                                          wts['wsp'], wts['bsp'], dftc, tm)
        ycat = _fft1(xr, xi, a_mat, twr, twi, bsz, n1)
        yb = _fft2(ycat, w2f, bsz, n1, min(n1 * D_B, 4096)).reshape(t, D_B)
        yc = _conv(l, yglu, wts['conv_w'], wts['conv_b'], wts['conv_ln_g'], wts['conv_ln_b'], seq, tq)
        x1, hn, ri, rw = _mix_out(l, xf, ya, yb, yc, gates, wts['w_out_a'], wts['w_out_b'], wts['w_out_c'],
                                  wts['w_o'], wts['norm_ffn'], wts['wr'], wts['br'], tm)
        dest, row_tok, row_w, block_e, n_used, n_blocks = _route_plan(ri, rw, t)
        yg = _experts(l, hn, row_tok, row_w, block_e, n_used, n_blocks, wts['w1'], wts['w3'], wts['w2'])
        xf = _combine(l, dest, yg, x1, pf, wts['norm_ple'], wts['w_ple_gate'], wts['w_ple_proj'],
                      wts['norm_final'], tq, l == depth - 1)
    return xf.reshape(bsz, seq, D_MODEL)


def _prepare_weights(norm_mix, w_in, gmlp_ln_g, gmlp_ln_b, w_spatial, b_spatial, conv_w, conv_b, conv_ln_g,
                     conv_ln_b, w_out_a, w_out_b, w_out_c, w_o, norm_ffn, w_router_group, b_router_group,
                     w_router_expert, b_router_expert, w1, w3, w2, norm_ple, w_ple_gate, w_ple_proj, norm_final):
    depth = w_in.shape[0]
    row = lambda a: a.reshape(depth, 1, -1)
    wsp = jnp.concatenate([w_spatial[:, 0::2], w_spatial[:, 1::2]], axis=-1).astype(BF16)
    bpair = jnp.stack([b_spatial[:, 0::2], b_spatial[:, 1::2]], axis=-1)
    bsp = jnp.repeat(bpair, A_HEAD, axis=-1)
    conv_wp = jnp.pad(conv_w, ((0, 0), (0, 1), (0, 0)))
    pad_g = jnp.zeros((depth, EXPERT_ROW0 - N_GROUPS, D_MODEL), F32)
    pad_e = jnp.zeros((depth, ROUTE_ROWS - EXPERT_ROW0 - N_EXPERTS, D_MODEL), F32)
    wr = jnp.concatenate([jnp.swapaxes(w_router_group, 1, 2), pad_g, jnp.swapaxes(w_router_expert, 1, 2), pad_e],
                         axis=1)
    br = jnp.concatenate([b_router_group, pad_g[:, :, 0], b_router_expert, pad_e[:, :, 0]], axis=1)[:, :, None]
    return dict(
        norm_mix=row(norm_mix), w_in=w_in.astype(BF16), gmlp_ln_g=row(gmlp_ln_g), gmlp_ln_b=row(gmlp_ln_b),
        wsp=wsp, bsp=bsp, conv_w=conv_wp, conv_b=row(conv_b), conv_ln_g=row(conv_ln_g), conv_ln_b=row(conv_ln_b),
        w_out_a=w_out_a.astype(BF16), w_out_b=w_out_b.astype(BF16), w_out_c=w_out_c.astype(BF16),
        w_o=w_o.astype(BF16), norm_ffn=row(norm_ffn), wr=wr, br=br,
        w1=w1.astype(BF16), w3=w3.astype(BF16), w2=w2.astype(BF16),
        norm_ple=row(norm_ple), w_ple_gate=w_ple_gate.astype(BF16), w_ple_proj=w_ple_proj.astype(BF16),
        norm_final=norm_final.reshape(1, D_MODEL))


def kernel(x_prompt, x_sample, p_prompt, p_sample, norm_mix, w_in, gmlp_ln_g, gmlp_ln_b, w_spatial, b_spatial, conv_w, conv_b, conv_ln_g, conv_ln_b, w_out_a, w_out_b, w_out_c, w_o, norm_ffn, w_router_group, b_router_group, w_router_expert, b_router_expert, w1, w3, w2, norm_ple, w_ple_gate, w_ple_proj, norm_final):
    wts = _prepare_weights(norm_mix, w_in, gmlp_ln_g, gmlp_ln_b, w_spatial, b_spatial, conv_w, conv_b, conv_ln_g,
                           conv_ln_b, w_out_a, w_out_b, w_out_c, w_o, norm_ffn, w_router_group, b_router_group,
                           w_router_expert, b_router_expert, w1, w3, w2, norm_ple, w_ple_gate, w_ple_proj,
                           norm_final)
    return (_run_group(x_prompt, p_prompt, wts), _run_group(x_sample, p_sample, wts))
```

```python
import functools
import math

import numpy as np
import jax
import jax.numpy as jnp
from jax import lax
from jax.experimental import pallas as pl
from jax.experimental.pallas import tpu as pltpu

D_MODEL = 1024
CHUNK = 128
A_GROUPS = 8
A_HEAD = 64
D_A = A_GROUPS * A_HEAD
B_GROUPS = 4
B_HEAD = 128
D_B = B_GROUPS * B_HEAD
D_C = 512
CONV_WIDTH = 31
CONV_HALO = 16
N_BRANCH = 3
O1 = 2 * D_A
O2 = O1 + D_B
O3 = O2 + 2 * D_C
IN_WIDTH = O3 + N_BRANCH * D_MODEL
N_GROUPS = 4
E_PER_GROUP = 8
N_EXPERTS = N_GROUPS * E_PER_GROUP
D_EXPERT = 512
MOE_BLOCK = 256
PLE_DIM = 256
EPS = 1e-6

LANES = 128
SUBLANES = 8
FFT_N2 = 128
ROUTE_ROWS = 128
EXPERT_ROW0 = 8
RANK_TILE = 1024
VMEM_LIMIT = 56 * 1024 * 1024

F32 = jnp.float32
BF16 = jnp.bfloat16


def _cparams(*sem):
    return pltpu.CompilerParams(dimension_semantics=sem, vmem_limit_bytes=VMEM_LIMIT)


def _resident(shape, index=None):
    idx = tuple(index) if index is not None else (0,) * len(shape)
    return pl.BlockSpec(shape, lambda *_: idx, pipeline_mode=pl.Buffered(1))


def _rms(x, g):
    return x * lax.rsqrt(jnp.mean(x * x, axis=-1, keepdims=True) + EPS) * g


def _ln(x, g, b):
    mu = jnp.mean(x, axis=-1, keepdims=True)
    xc = x - mu
    return xc * lax.rsqrt(jnp.mean(xc * xc, axis=-1, keepdims=True) + EPS) * g + b


def _sigmoid(x):
    return 1.0 / (1.0 + jnp.exp(-x))


def _dot(a, b):
    return jnp.dot(a, b, preferred_element_type=F32)


def _mix_in_kernel(x_ref, nm_ref, w_ref, lng_ref, lnb_ref, wsp_ref, bsp_ref, dft_ref,
                   ya_ref, xr_ref, xi_ref, yglu_ref, gates_ref, *, tm):
    hb = _rms(x_ref[...], nm_ref[...]).astype(BF16)

    def proj(lo, hi):
        return _dot(hb, w_ref[:, lo:hi])

    za = proj(0, O1)
    za = 0.5 * za * (1.0 + lax.erf(za * (1.0 / math.sqrt(2.0))))
    u = za[:, :D_A]
    v = _ln(za[:, D_A:], lng_ref[...], lnb_ref[...])
    n_chunks = tm // CHUNK
    low_half = lax.broadcasted_iota(jnp.int32, (CHUNK, LANES), 1) < A_HEAD
    for j in range(A_GROUPS // 2):
        cols = slice(j * LANES, (j + 1) * LANES)
        pieces = []
        for c in range(n_chunks):
            vp = v[c * CHUNK:(c + 1) * CHUNK, cols]
            pieces.append(jnp.concatenate([jnp.where(low_half, vp, 0.0), jnp.where(low_half, 0.0, vp)], axis=0))
        rhs = jnp.concatenate(pieces, axis=1).astype(BF16)
        sv = _dot(wsp_ref[j], rhs)
        for c in range(n_chunks):
            rows = slice(c * CHUNK, (c + 1) * CHUNK)
            svc = sv[:, c * LANES:(c + 1) * LANES] + bsp_ref[j]
            ya_ref[rows, cols] = (u[rows, cols] * svc).astype(ya_ref.dtype)

    zb = proj(O1, O2).astype(BF16)
    for g in range(B_GROUPS):
        cols = slice(g * B_HEAD, (g + 1) * B_HEAD)
        c = _dot(zb[:, cols], dft_ref[...])
        xr_ref[:, cols] = c[:, :B_HEAD]
        xi_ref[:, cols] = c[:, B_HEAD:]

    zc = proj(O2, O3)
    yglu_ref[...] = zc[:, :D_C] * _sigmoid(zc[:, D_C:])

    for b in range(N_BRANCH):
        cols = slice(b * D_MODEL, (b + 1) * D_MODEL)
        gates_ref[:, cols] = _sigmoid(proj(O3 + b * D_MODEL, O3 + (b + 1) * D_MODEL)).astype(gates_ref.dtype)


def _mix_in(layer, x, nm, w_in, lng, lnb, wsp, bsp, dftc, tm):
    t = x.shape[0]

    def row(w):
        return pl.BlockSpec((tm, w), lambda i: (i, 0))

    def lay(shape):
        return pl.BlockSpec((None,) + shape, lambda i: (layer,) + (0,) * len(shape), pipeline_mode=pl.Buffered(1))

    return pl.pallas_call(
        functools.partial(_mix_in_kernel, tm=tm),
        grid=(t // tm,),
        in_specs=[row(D_MODEL), lay((1, D_MODEL)), lay((D_MODEL, IN_WIDTH)), lay((1, D_A)), lay((1, D_A)),
                  lay((A_GROUPS // 2, CHUNK, 2 * CHUNK)), lay((A_GROUPS // 2, CHUNK, LANES)),
                  _resident((B_HEAD, 2 * B_HEAD))],
        out_specs=[row(D_A), row(D_B), row(D_B), row(D_C), row(N_BRANCH * D_MODEL)],
        out_shape=[jax.ShapeDtypeStruct((t, D_A), BF16), jax.ShapeDtypeStruct((t, D_B), F32),
                   jax.ShapeDtypeStruct((t, D_B), F32), jax.ShapeDtypeStruct((t, D_C), F32),
                   jax.ShapeDtypeStruct((t, N_BRANCH * D_MODEL), BF16)],
        compiler_params=_cparams("parallel"),
        name="mix_in",
    )(x, nm, w_in, lng, lnb, wsp, bsp, dftc)


def _fft1_kernel(xr_ref, xi_ref, a_ref, twr_ref, twi_ref, y_ref, *, n1):
    for j in range(SUBLANES):
        x = jnp.concatenate([xr_ref[:, j, :], xi_ref[:, j, :]], axis=0).astype(BF16)
        y = _dot(a_ref[...], x)
        yr, yi = y[:n1], y[n1:]
        tr, ti = twr_ref[j], twi_ref[j]
        y_ref[0, j] = (yr * tr - yi * ti).astype(y_ref.dtype)
        y_ref[1, j] = (yr * ti + yi * tr).astype(y_ref.dtype)


def _fft1(xr, xi, a_mat, twr, twi, bsz, n1):
    n2 = FFT_N2
    xr4 = xr.reshape(bsz, n1, n2, D_B)
    xi4 = xi.reshape(bsz, n1, n2, D_B)
    blk = pl.BlockSpec((None, n1, SUBLANES, D_B), lambda b, j: (b, 0, j, 0))
    tw = pl.BlockSpec((SUBLANES, n1, 1), lambda b, j: (j, 0, 0))
    return pl.pallas_call(
        functools.partial(_fft1_kernel, n1=n1),
        grid=(bsz, n2 // SUBLANES),
        in_specs=[blk, blk, _resident((2 * n1, 2 * n1)), tw, tw],
        out_specs=pl.BlockSpec((None, 2, SUBLANES, n1, D_B), lambda b, j: (b, 0, j, 0, 0)),
        out_shape=jax.ShapeDtypeStruct((bsz, 2, n2, n1, D_B), BF16),
        compiler_params=_cparams("parallel", "parallel"),
        name="fft1",
    )(xr4, xi4, a_mat, twr, twi)


def _fft2_kernel(y_ref, w_ref, o_ref):
    y = jnp.concatenate([y_ref[0], y_ref[1]], axis=0)
    o_ref[...] = _dot(w_ref[...], y).astype(o_ref.dtype)


def _fft2(ycat, w2, bsz, n1, cols_blk):
    n2 = FFT_N2
    width = n1 * D_B
    y4 = ycat.reshape(bsz, 2, n2, width)
    return pl.pallas_call(
        _fft2_kernel,
        grid=(bsz, width // cols_blk),
        in_specs=[pl.BlockSpec((None, 2, n2, cols_blk), lambda b, c: (b, 0, 0, c)), _resident((n2, 2 * n2))],
        out_specs=pl.BlockSpec((None, n2, cols_blk), lambda b, c: (b, 0, c)),
        out_shape=jax.ShapeDtypeStruct((bsz, n2, width), BF16),
        compiler_params=_cparams("parallel", "parallel"),
        name="fft2",
    )(y4, w2)


def _dft_constants(n1):
    n2 = FFT_N2
    k = np.arange(B_HEAD)
    ang = 2.0 * np.pi * np.outer(k, k) / B_HEAD
    dftc = np.concatenate([np.cos(ang), -np.sin(ang)], axis=1) / math.sqrt(B_HEAD)
    k1 = np.arange(n1)
    ang1 = 2.0 * np.pi * np.outer(k1, k1) / n1
    ar, ai = np.cos(ang1) / math.sqrt(n1), -np.sin(ang1) / math.sqrt(n1)
    a_mat = np.block([[ar, -ai], [ai, ar]])
    n2i = np.arange(n2)
    angt = 2.0 * np.pi * np.outer(n2i, k1) / (n1 * n2)
    twr, twi = np.cos(angt)[:, :, None], -np.sin(angt)[:, :, None]
    ang2 = 2.0 * np.pi * np.outer(n2i, n2i) / n2
    w2 = np.concatenate([np.cos(ang2), np.sin(ang2)], axis=1) / math.sqrt(n2)
    return (jnp.asarray(dftc, BF16), jnp.asarray(a_mat, BF16), jnp.asarray(twr, F32), jnp.asarray(twi, F32),
            jnp.asarray(w2, BF16))


CONV_ROWS = 32


def _conv_kernel(cur_ref, prev_ref, next_ref, w_ref, b_ref, g_ref, bb_ref, o_ref, sh_ref, *, tc, seq):
    i = pl.program_id(0)
    first = (i * tc) % seq == 0
    last = ((i + 1) * tc) % seq == 0
    n_ext = tc + 2 * CONV_HALO
    ext = jnp.concatenate([jnp.where(first, 0.0, prev_ref[...]), cur_ref[...],
                           jnp.where(last, 0.0, next_ref[...])], axis=0)
    sh_ref[0] = ext
    for s in range(1, SUBLANES):
        sh_ref[s] = pltpu.roll(ext, n_ext - s, axis=0)
    base = CONV_HALO - CONV_WIDTH // 2
    for r in range(tc // CONV_ROWS):
        acc = jnp.zeros((CONV_ROWS, D_C), F32)
        for k in range(CONV_WIDTH):
            s = (base + k) % SUBLANES
            lo = r * CONV_ROWS + base + k - s
            acc = acc + w_ref[k:k + 1, :] * sh_ref[s, lo:lo + CONV_ROWS, :]
        y = _ln(acc + b_ref[...], g_ref[...], bb_ref[...])
        o_ref[r * CONV_ROWS:(r + 1) * CONV_ROWS] = (y * _sigmoid(y)).astype(o_ref.dtype)


def _conv(layer, yglu, conv_w, conv_b, lng, lnb, seq, tc):
    t = yglu.shape[0]
    hb = tc // CONV_HALO
    n_halo = t // CONV_HALO

    def lay(shape):
        return pl.BlockSpec((None,) + shape, lambda i: (layer,) + (0,) * len(shape), pipeline_mode=pl.Buffered(1))

    return pl.pallas_call(
        functools.partial(_conv_kernel, tc=tc, seq=seq),
        grid=(t // tc,),
        in_specs=[pl.BlockSpec((tc, D_C), lambda i: (i, 0)),
                  pl.BlockSpec((CONV_HALO, D_C), lambda i: (jnp.maximum(i * hb - 1, 0), 0)),
                  pl.BlockSpec((CONV_HALO, D_C), lambda i: (jnp.minimum((i + 1) * hb, n_halo - 1), 0)),
                  lay((CONV_WIDTH + 1, D_C)), lay((1, D_C)), lay((1, D_C)), lay((1, D_C))],
        out_specs=pl.BlockSpec((tc, D_C), lambda i: (i, 0)),
        out_shape=jax.ShapeDtypeStruct((t, D_C), BF16),
        scratch_shapes=[pltpu.VMEM((SUBLANES, tc + 2 * CONV_HALO, D_C), F32)],
        compiler_params=_cparams("parallel"),
        name="conv",
    )(yglu, yglu, yglu, conv_w, conv_b, lng, lnb)


def _mix_out_kernel(x_ref, ya_ref, yb_ref, yc_ref, gt_ref, wa_ref, wb_ref, wc_ref, wo_ref, nf_ref,
                    wrh_ref, wrl_ref, br_ref, x1_ref, hn_ref, ri_ref, rw_ref, *, tm):
    m = gt_ref[:, 0:D_MODEL].astype(F32) * _dot(ya_ref[...], wa_ref[...])
    m = m + gt_ref[:, D_MODEL:2 * D_MODEL].astype(F32) * _dot(yb_ref[...], wb_ref[...])
    m = m + gt_ref[:, 2 * D_MODEL:3 * D_MODEL].astype(F32) * _dot(yc_ref[...], wc_ref[...])
    x1 = x_ref[...] + _dot(m.astype(BF16), wo_ref[...])
    x1_ref[...] = x1
    hn = _rms(x1, nf_ref[...])
    hn_ref[...] = hn

    hn_hi = hn.astype(BF16)
    hn_lo = (hn - hn_hi.astype(F32)).astype(BF16)
    nt = (((1,), (1,)), ((), ()))
    lg = (lax.dot_general(wrh_ref[...], hn_hi, nt, preferred_element_type=F32)
          + lax.dot_general(wrh_ref[...], hn_lo, nt, preferred_element_type=F32)
          + lax.dot_general(wrl_ref[...], hn_hi, nt, preferred_element_type=F32)) + br_ref[...]
    gl = [lg[g:g + 1, :] for g in range(N_GROUPS)]
    best = gl[0]
    gsel = jnp.zeros((1, tm), jnp.int32)
    for g in range(1, N_GROUPS):
        better = gl[g] > best
        best = jnp.where(better, gl[g], best)
        gsel = jnp.where(better, g, gsel)
    denom = jnp.exp(gl[0] - best)
    for g in range(1, N_GROUPS):
        denom = denom + jnp.exp(gl[g] - best)
    pg = 1.0 / denom
    es = lg[EXPERT_ROW0:EXPERT_ROW0 + E_PER_GROUP, :]
    for g in range(1, N_GROUPS):
        lo = EXPERT_ROW0 + g * E_PER_GROUP
        es = jnp.where(gsel == g, lg[lo:lo + E_PER_GROUP, :], es)
    slot = lax.broadcasted_iota(jnp.int32, (E_PER_GROUP, tm), 0)
    v0 = jnp.max(es, axis=0, keepdims=True)
    i0 = jnp.min(jnp.where(es == v0, slot, E_PER_GROUP), axis=0, keepdims=True)
    es1 = jnp.where(slot == i0, -jnp.inf, es)
    v1 = jnp.max(es1, axis=0, keepdims=True)
    i1 = jnp.min(jnp.where(es1 == v1, slot, E_PER_GROUP), axis=0, keepdims=True)
    ex = jnp.exp(v1 - v0)
    p0 = 1.0 / (1.0 + ex)
    p1 = ex / (1.0 + ex)
    row = lax.broadcasted_iota(jnp.int32, (SUBLANES, tm), 0)
    e0 = gsel * E_PER_GROUP + i0
    e1 = gsel * E_PER_GROUP + i1
    ri_ref[...] = jnp.where(row == 0, e0, jnp.where(row == 1, e1, 0))
    wrow = lax.broadcasted_iota(jnp.int32, (LANES, tm), 0)
    rw_ref[...] = jnp.where(wrow == 0, pg * p0, jnp.where(wrow == 1, pg * p1, 0.0)).T


def _mix_out(layer, x, ya, yb, yc, gates, wa, wb, wc, wo, nf, wr_hi, wr_lo, br, tm):
    t = x.shape[0]

    def row(w):
        return pl.BlockSpec((tm, w), lambda i: (i, 0))

    def lay(shape):
        return pl.BlockSpec((None,) + shape, lambda i: (layer,) + (0,) * len(shape), pipeline_mode=pl.Buffered(1))

    return pl.pallas_call(
        functools.partial(_mix_out_kernel, tm=tm),
        grid=(t // tm,),
        in_specs=[row(D_MODEL), row(D_A), row(D_B), row(D_C), row(N_BRANCH * D_MODEL),
                  lay((D_A, D_MODEL)), lay((D_B, D_MODEL)), lay((D_C, D_MODEL)), lay((D_MODEL, D_MODEL)),
                  lay((1, D_MODEL)), lay((ROUTE_ROWS, D_MODEL)), lay((ROUTE_ROWS, D_MODEL)), lay((ROUTE_ROWS, 1))],
        out_specs=[row(D_MODEL), row(D_MODEL), pl.BlockSpec((SUBLANES, tm), lambda i: (0, i)), row(LANES)],
        out_shape=[jax.ShapeDtypeStruct((t, D_MODEL), F32), jax.ShapeDtypeStruct((t, D_MODEL), F32),
                   jax.ShapeDtypeStruct((SUBLANES, t), jnp.int32), jax.ShapeDtypeStruct((t, LANES), F32)],
        compiler_params=_cparams("parallel"),
        name="mix_out",
    )(x, ya, yb, yc, gates, wa, wb, wc, wo, nf, wr_hi, wr_lo, br)


def _rank_kernel(e_ref, u_ref, rank_ref, cnt_ref, run_ref):
    k = pl.program_id(0)
    j = pl.program_id(1)

    @pl.when((k == 0) & (j == 0))
    def _():
        run_ref[...] = jnp.zeros_like(run_ref)

    tl = e_ref.shape[1]
    e = jnp.where(k == 0, e_ref[0:1, :], e_ref[1:2, :])
    onehot = lax.broadcasted_iota(jnp.int32, (N_EXPERTS, tl), 0) == e
    csum = _dot(jnp.where(onehot, 1.0, 0.0).astype(BF16), u_ref[...])
    base = run_ref[:, 0:1]
    rank = jnp.sum(jnp.where(onehot, csum - 1.0 + base, 0.0), axis=0, keepdims=True)
    rank_ref[...] = rank.astype(jnp.int32)
    run_ref[...] = run_ref[...] + csum[:, tl - 1:tl]
    cnt_ref[...] = run_ref[...]


def _route_plan(ri, t):
    tl = RANK_TILE if t % RANK_TILE == 0 else CHUNK
    upper = jnp.asarray(np.triu(np.ones((tl, tl), np.float32)), BF16)
    rank, cnt = pl.pallas_call(
        _rank_kernel,
        grid=(2, t // tl),
        in_specs=[pl.BlockSpec((SUBLANES, tl), lambda k, j: (0, j)), _resident((tl, tl))],
        out_specs=[pl.BlockSpec((None, 1, tl), lambda k, j: (k, 0, j)), _resident((N_EXPERTS, LANES))],
        out_shape=[jax.ShapeDtypeStruct((2, 1, t), jnp.int32), jax.ShapeDtypeStruct((N_EXPERTS, LANES), F32)],
        scratch_shapes=[pltpu.VMEM((N_EXPERTS, LANES), F32)],
        compiler_params=_cparams("arbitrary", "arbitrary"),
        name="rank",
    )(ri, upper)
    counts = cnt[:, 0].astype(jnp.int32)
    padded = (counts + MOE_BLOCK - 1) // MOE_BLOCK * MOE_BLOCK
    pad_end = jnp.cumsum(padded)
    pad_start = pad_end - padded
    experts = jnp.arange(N_EXPERTS, dtype=jnp.int32)
    dest = rank[:, 0, :] + jnp.sum(jnp.where(ri[:2, :, None] == experts, pad_start, 0), axis=-1)
    n_blocks = 2 * t // MOE_BLOCK + N_EXPERTS
    first_row = jnp.arange(n_blocks, dtype=jnp.int32) * MOE_BLOCK
    block_e = jnp.minimum(jnp.sum((pad_end[None, :] <= first_row[:, None]).astype(jnp.int32), axis=1), N_EXPERTS - 1)
    n_used = (pad_end[-1] // MOE_BLOCK).astype(jnp.int32).reshape(1)
    tail = pad_end[-1] + experts * MOE_BLOCK
    zero_blk = jnp.concatenate([jnp.where(padded > 0, pad_end - MOE_BLOCK, -1),
                                jnp.where(tail < n_blocks * MOE_BLOCK, tail, -1)]).astype(jnp.int32)
    return dest.astype(jnp.int32), block_e.astype(jnp.int32), n_used, zero_blk, n_blocks


DISPATCH_UNROLL = 8


def _dispatch_kernel(zb_ref, d_ref, hn_ref, xg_hbm, zero_ref, sem, *, td):
    i = pl.program_id(0)

    def zero_copy(e):
        start = pl.multiple_of(zb_ref[e], MOE_BLOCK)
        return pltpu.make_async_copy(zero_ref, xg_hbm.at[pl.ds(start, MOE_BLOCK)], sem.at[1])

    @pl.when(i == 0)
    def _():
        zero_ref[...] = jnp.zeros_like(zero_ref)
        for e in range(2 * N_EXPERTS):
            @pl.when(zb_ref[e] >= 0)
            def _():
                zero_copy(e).start()
        for e in range(2 * N_EXPERTS):
            @pl.when(zb_ref[e] >= 0)
            def _():
                zero_copy(e).wait()

    def body(c, carry):
        for u in range(DISPATCH_UNROLL):
            r = c * DISPATCH_UNROLL + u
            src = hn_ref.at[pl.ds(r, 1)]
            pltpu.make_async_copy(src, xg_hbm.at[pl.ds(d_ref[0, 0, r], 1)], sem.at[0]).start()
            pltpu.make_async_copy(src, xg_hbm.at[pl.ds(d_ref[0, 0, td + r], 1)], sem.at[0]).start()
        return carry

    lax.fori_loop(0, td // DISPATCH_UNROLL, body, 0)
    for _ in range(2):
        pltpu.make_async_copy(hn_ref, xg_hbm.at[pl.ds(0, td)], sem.at[0]).wait()


def _dest_tiles(dest, n_tiles, tile):
    return dest.reshape(2, n_tiles, tile).transpose(1, 0, 2).reshape(n_tiles, 1, 2 * tile)


def _dispatch(hn, dest, zero_blk, n_blocks, td):
    t = hn.shape[0]
    n_tiles = t // td
    grid_spec = pltpu.PrefetchScalarGridSpec(
        num_scalar_prefetch=1,
        grid=(n_tiles,),
        in_specs=[pl.BlockSpec((1, 1, 2 * td), lambda i, zb: (i, 0, 0), memory_space=pltpu.SMEM),
                  pl.BlockSpec((td, D_MODEL), lambda i, zb: (i, 0))],
        out_specs=pl.BlockSpec(memory_space=pl.ANY),
        scratch_shapes=[pltpu.VMEM((MOE_BLOCK, D_MODEL), F32), pltpu.SemaphoreType.DMA((2,))],
    )
    return pl.pallas_call(
        functools.partial(_dispatch_kernel, td=td),
        grid_spec=grid_spec,
        out_shape=jax.ShapeDtypeStruct((n_blocks * MOE_BLOCK, D_MODEL), F32),
        compiler_params=_cparams("arbitrary"),
        name="dispatch",
    )(zero_blk, _dest_tiles(dest, n_tiles, td), hn)


def _expert_kernel(be_ref, nu_ref, x_ref, w1_ref, w3_ref, w2_ref, o_ref):
    i = pl.program_id(0)

    @pl.when(i < nu_ref[0])
    def _():
        xb = x_ref[...].astype(BF16)
        h1 = _dot(xb, w1_ref[...])
        h3 = _dot(xb, w3_ref[...])
        act = (h1 * _sigmoid(h1) * h3).astype(BF16)
        o_ref[...] = _dot(act, w2_ref[...])

    @pl.when(i >= nu_ref[0])
    def _():
        o_ref[...] = jnp.zeros_like(o_ref)


def _experts(layer, xg, block_e, n_used, n_blocks, w1, w3, w2):
    grid_spec = pltpu.PrefetchScalarGridSpec(
        num_scalar_prefetch=2,
        grid=(n_blocks,),
        in_specs=[
            pl.BlockSpec((MOE_BLOCK, D_MODEL), lambda i, be, nu: (jnp.minimum(i, nu[0] - 1), 0)),
            pl.BlockSpec((None, None, D_MODEL, D_EXPERT), lambda i, be, nu: (layer, be[i], 0, 0)),
            pl.BlockSpec((None, None, D_MODEL, D_EXPERT), lambda i, be, nu: (layer, be[i], 0, 0)),
            pl.BlockSpec((None, None, D_EXPERT, D_MODEL), lambda i, be, nu: (layer, be[i], 0, 0)),
        ],
        out_specs=pl.BlockSpec((MOE_BLOCK, D_MODEL), lambda i, be, nu: (i, 0)),
    )
    return pl.pallas_call(
        _expert_kernel,
        grid_spec=grid_spec,
        out_shape=jax.ShapeDtypeStruct((n_blocks * MOE_BLOCK, D_MODEL), F32),
        compiler_params=_cparams("arbitrary"),
        name="experts",
    )(block_e, n_used, xg, w1, w3, w2)


def _combine_kernel(d_ref, dn_ref, yg_hbm, x1_ref, rw_ref, p_ref, np_ref, wg_ref, wp_ref, nfin_ref, o_ref, gbuf, sem,
                    *, tq, n_tiles, final):
    i = pl.program_id(0)
    slot = i % 2

    def issue(dref, s):
        def body(c, carry):
            for u in range(DISPATCH_UNROLL):
                r = c * DISPATCH_UNROLL + u
                pltpu.make_async_copy(yg_hbm.at[pl.ds(dref[0, 0, r], 1)], gbuf.at[s, pl.ds(r, 1)],
                                      sem.at[s]).start()
            return carry
        lax.fori_loop(0, 2 * tq // DISPATCH_UNROLL, body, 0)

    def wait(s):
        pltpu.make_async_copy(yg_hbm.at[pl.ds(0, 2 * tq)], gbuf.at[s], sem.at[s]).wait()

    @pl.when(i == 0)
    def _():
        issue(d_ref, 0)

    issue(dn_ref, 1 - slot)
    wait(slot)
    moe = rw_ref[:, 0:1] * gbuf[slot, 0:tq, :] + rw_ref[:, 1:2] * gbuf[slot, tq:2 * tq, :]
    x2 = x1_ref[...] + moe
    g = _sigmoid(_dot(_rms(x2, np_ref[...]).astype(BF16), wg_ref[...]))
    x3 = x2 + g * _dot(p_ref[...].astype(BF16), wp_ref[...])
    if final:
        x3 = _rms(x3, nfin_ref[...])
    o_ref[...] = x3

    @pl.when(i == n_tiles - 1)
    def _():
        wait(1 - slot)


def _combine(layer, dest, yg, x1, rw, p, norm_ple, wg, wp, nfin, tq, final):
    t = x1.shape[0]
    n_tiles = t // tq
    d3 = _dest_tiles(dest, n_tiles, tq)

    def lay(shape):
        return pl.BlockSpec((None,) + shape, lambda i: (layer,) + (0,) * len(shape), pipeline_mode=pl.Buffered(1))

    return pl.pallas_call(
        functools.partial(_combine_kernel, tq=tq, n_tiles=n_tiles, final=final),
        grid=(n_tiles,),
        in_specs=[
            pl.BlockSpec((1, 1, 2 * tq), lambda i: (i, 0, 0), memory_space=pltpu.SMEM),
            pl.BlockSpec((1, 1, 2 * tq), lambda i: (jnp.minimum(i + 1, n_tiles - 1), 0, 0), memory_space=pltpu.SMEM),
            pl.BlockSpec(memory_space=pl.ANY),
            pl.BlockSpec((tq, D_MODEL), lambda i: (i, 0)),
            pl.BlockSpec((tq, LANES), lambda i: (i, 0)),
            pl.BlockSpec((None, tq, PLE_DIM), lambda i: (layer, i, 0)),
            lay((1, D_MODEL)), lay((D_MODEL, D_MODEL)), lay((PLE_DIM, D_MODEL)), _resident((1, D_MODEL)),
        ],
        out_specs=pl.BlockSpec((tq, D_MODEL), lambda i: (i, 0)),
        out_shape=jax.ShapeDtypeStruct((t, D_MODEL), F32),
        scratch_shapes=[pltpu.VMEM((2, 2 * tq, D_MODEL), F32), pltpu.SemaphoreType.DMA((2,))],
        compiler_params=_cparams("arbitrary"),
        name="combine",
    )(d3, d3, yg, x1, rw, p, norm_ple, wg, wp, nfin)


def _tile(t, want):
    return want if t % want == 0 else CHUNK


def _run_group(x, p, wts):
    bsz, seq, _ = x.shape
    depth = p.shape[0]
    t = bsz * seq
    n1 = seq // FFT_N2
    tm = _tile(t, 512)
    tq = _tile(t, 256)
    dftc, a_mat, twr, twi, w2f = _dft_constants(n1)
    xf = x.reshape(t, D_MODEL)
    pf = p.reshape(depth, t, PLE_DIM)
    for l in range(depth):
        ya, xr, xi, yglu, gates = _mix_in(l, xf, wts['norm_mix'], wts['w_in'], wts['gmlp_ln_g'], wts['gmlp_ln_b'],
                                          wts['wsp'], wts['bsp'], dftc, tm)
        ycat = _fft1(xr, xi, a_mat, twr, twi, bsz, n1)
        yb = _fft2(ycat, w2f, bsz, n1, min(n1 * D_B, 4096)).reshape(t, D_B)
        yc = _conv(l, yglu, wts['conv_w'], wts['conv_b'], wts['conv_ln_g'], wts['conv_ln_b'], seq, tq)
        x1, hn, ri, rw = _mix_out(l, xf, ya, yb, yc, gates, wts['w_out_a'], wts['w_out_b'], wts['w_out_c'],
                                  wts['w_o'], wts['norm_ffn'], wts['wr_hi'], wts['wr_lo'], wts['br'], tm)
        dest, block_e, n_used, zero_blk, n_blocks = _route_plan(ri, t)
        xg = _dispatch(hn, dest, zero_blk, n_blocks, tm)
        yg = _experts(l, xg, block_e, n_used, n_blocks, wts['w1'], wts['w3'], wts['w2'])
        xf = _combine(l, dest, yg, x1, rw, pf, wts['norm_ple'], wts['w_ple_gate'], wts['w_ple_proj'],
                      wts['norm_final'], tq, l == depth - 1)
    return xf.reshape(bsz, seq, D_MODEL)


def _prepare_weights(norm_mix, w_in, gmlp_ln_g, gmlp_ln_b, w_spatial, b_spatial, conv_w, conv_b, conv_ln_g,
                     conv_ln_b, w_out_a, w_out_b, w_out_c, w_o, norm_ffn, w_router_group, b_router_group,
                     w_router_expert, b_router_expert, w1, w3, w2, norm_ple, w_ple_gate, w_ple_proj, norm_final):
    depth = w_in.shape[0]
    row = lambda a: a.reshape(depth, 1, -1)
    wsp = jnp.concatenate([w_spatial[:, 0::2], w_spatial[:, 1::2]], axis=-1).astype(BF16)
    bpair = jnp.stack([b_spatial[:, 0::2], b_spatial[:, 1::2]], axis=-1)
    bsp = jnp.repeat(bpair, A_HEAD, axis=-1)
    conv_wp = jnp.pad(conv_w, ((0, 0), (0, 1), (0, 0)))
    pad_g = jnp.zeros((depth, EXPERT_ROW0 - N_GROUPS, D_MODEL), F32)
    pad_e = jnp.zeros((depth, ROUTE_ROWS - EXPERT_ROW0 - N_EXPERTS, D_MODEL), F32)
    wr = jnp.concatenate([jnp.swapaxes(w_router_group, 1, 2), pad_g, jnp.swapaxes(w_router_expert, 1, 2), pad_e],
                         axis=1)
    br = jnp.concatenate([b_router_group, pad_g[:, :, 0], b_router_expert, pad_e[:, :, 0]], axis=1)[:, :, None]
    wr_hi = wr.astype(BF16)
    wr_lo = (wr - wr_hi.astype(F32)).astype(BF16)
    return dict(
        norm_mix=row(norm_mix), w_in=w_in.astype(BF16), gmlp_ln_g=row(gmlp_ln_g), gmlp_ln_b=row(gmlp_ln_b),
        wsp=wsp, bsp=bsp, conv_w=conv_wp, conv_b=row(conv_b), conv_ln_g=row(conv_ln_g), conv_ln_b=row(conv_ln_b),
        w_out_a=w_out_a.astype(BF16), w_out_b=w_out_b.astype(BF16), w_out_c=w_out_c.astype(BF16),
        w_o=w_o.astype(BF16), norm_ffn=row(norm_ffn), wr_hi=wr_hi, wr_lo=wr_lo, br=br,
        w1=w1.astype(BF16), w3=w3.astype(BF16), w2=w2.astype(BF16),
        norm_ple=row(norm_ple), w_ple_gate=w_ple_gate.astype(BF16), w_ple_proj=w_ple_proj.astype(BF16),
        norm_final=norm_final.reshape(1, D_MODEL))


def kernel(x_prompt, x_sample, p_prompt, p_sample, norm_mix, w_in, gmlp_ln_g, gmlp_ln_b, w_spatial, b_spatial, conv_w, conv_b, conv_ln_g, conv_ln_b, w_out_a, w_out_b, w_out_c, w_o, norm_ffn, w_router_group, b_router_group, w_router_expert, b_router_expert, w1, w3, w2, norm_ple, w_ple_gate, w_ple_proj, norm_final):
    wts = _prepare_weights(norm_mix, w_in, gmlp_ln_g, gmlp_ln_b, w_spatial, b_spatial, conv_w, conv_b, conv_ln_g,
                           conv_ln_b, w_out_a, w_out_b, w_out_c, w_o, norm_ffn, w_router_group, b_router_group,
                           w_router_expert, b_router_expert, w1, w3, w2, norm_ple, w_ple_gate, w_ple_proj,
                           norm_final)
    return (_run_group(x_prompt, p_prompt, wts), _run_group(x_sample, p_sample, wts))
```

```python
import functools
import math

import numpy as np
import jax
import jax.numpy as jnp
from jax import lax
from jax.experimental import pallas as pl
from jax.experimental.pallas import tpu as pltpu

D_MODEL = 1024
CHUNK = 128
A_GROUPS = 8
A_HEAD = 64
D_A = A_GROUPS * A_HEAD
B_GROUPS = 4
B_HEAD = 128
D_B = B_GROUPS * B_HEAD
D_C = 512
CONV_WIDTH = 31
CONV_HALO = 16
N_BRANCH = 3
O1 = 2 * D_A
O2 = O1 + D_B
O3 = O2 + 2 * D_C
IN_WIDTH = O3 + N_BRANCH * D_MODEL
N_GROUPS = 4
E_PER_GROUP = 8
N_EXPERTS = N_GROUPS * E_PER_GROUP
D_EXPERT = 512
MOE_BLOCK = 256
PLE_DIM = 256
EPS = 1e-6

LANES = 128
SUBLANES = 8
FFT_N2 = 128
ROUTE_ROWS = 128
EXPERT_ROW0 = 8
RANK_TILE = 1024
VMEM_LIMIT = 56 * 1024 * 1024

F32 = jnp.float32
BF16 = jnp.bfloat16


def _cparams(*sem):
    return pltpu.CompilerParams(dimension_semantics=sem, vmem_limit_bytes=VMEM_LIMIT)


def _resident(shape, index=None):
    idx = tuple(index) if index is not None else (0,) * len(shape)
    return pl.BlockSpec(shape, lambda *_: idx, pipeline_mode=pl.Buffered(1))


def _rms(x, g):
    return x * lax.rsqrt(jnp.mean(x * x, axis=-1, keepdims=True) + EPS) * g


def _ln(x, g, b):
    mu = jnp.mean(x, axis=-1, keepdims=True)
    xc = x - mu
    return xc * lax.rsqrt(jnp.mean(xc * xc, axis=-1, keepdims=True) + EPS) * g + b


def _sigmoid(x):
    return 1.0 / (1.0 + jnp.exp(-x))


def _dot(a, b):
    return jnp.dot(a, b, preferred_element_type=F32)


def _pack_pair(a, b):
    hi = lax.bitcast_convert_type(a.astype(BF16).astype(F32), jnp.uint32)
    lo = lax.bitcast_convert_type(b.astype(BF16).astype(F32), jnp.uint32)
    return hi | (lo >> 16)


def _unpack_pair(p):
    a = lax.bitcast_convert_type(p & jnp.uint32(0xFFFF0000), F32)
    b = lax.bitcast_convert_type(p << 16, F32)
    return a, b


def _pack_row_halves(x):
    n = x.shape[1] // 2
    return _pack_pair(x[:, :n], x[:, n:])


def _unpack_row_halves(p):
    return jnp.concatenate(_unpack_pair(p), axis=1)


def _mix_in_kernel(x_ref, nm_ref, w_ref, lng_ref, lnb_ref, wsp_ref, bsp_ref, dft_ref,
                   ya_ref, xri_ref, yglu_ref, gates_ref, *, tm):
    hb = _rms(x_ref[...], nm_ref[...]).astype(BF16)

    def proj(lo, hi):
        return _dot(hb, w_ref[:, lo:hi])

    za = proj(0, O1)
    za = 0.5 * za * (1.0 + lax.erf(za * (1.0 / math.sqrt(2.0))))
    u = za[:, :D_A]
    v = _ln(za[:, D_A:], lng_ref[...], lnb_ref[...])
    n_chunks = tm // CHUNK
    low_half = lax.broadcasted_iota(jnp.int32, (CHUNK, LANES), 1) < A_HEAD
    for j in range(A_GROUPS // 2):
        cols = slice(j * LANES, (j + 1) * LANES)
        pieces = []
        for c in range(n_chunks):
            vp = v[c * CHUNK:(c + 1) * CHUNK, cols]
            pieces.append(jnp.concatenate([jnp.where(low_half, vp, 0.0), jnp.where(low_half, 0.0, vp)], axis=0))
        rhs = jnp.concatenate(pieces, axis=1).astype(BF16)
        sv = _dot(wsp_ref[j], rhs)
        for c in range(n_chunks):
            rows = slice(c * CHUNK, (c + 1) * CHUNK)
            svc = sv[:, c * LANES:(c + 1) * LANES] + bsp_ref[j]
            ya_ref[rows, cols] = (u[rows, cols] * svc).astype(ya_ref.dtype)

    zb = proj(O1, O2).astype(BF16)
    for g in range(B_GROUPS):
        cols = slice(g * B_HEAD, (g + 1) * B_HEAD)
        c = _dot(zb[:, cols], dft_ref[...])
        xri_ref[:, cols] = _pack_pair(c[:, :B_HEAD], c[:, B_HEAD:])

    zc = proj(O2, O3)
    yglu_ref[...] = zc[:, :D_C] * _sigmoid(zc[:, D_C:])

    for b in range(N_BRANCH):
        cols = slice(b * D_MODEL, (b + 1) * D_MODEL)
        gates_ref[:, cols] = _sigmoid(proj(O3 + b * D_MODEL, O3 + (b + 1) * D_MODEL)).astype(gates_ref.dtype)


def _mix_in(layer, x, nm, w_in, lng, lnb, wsp, bsp, dftc, tm):
    t = x.shape[0]

    def row(w):
        return pl.BlockSpec((tm, w), lambda i: (i, 0))

    def lay(shape):
        return pl.BlockSpec((None,) + shape, lambda i: (layer,) + (0,) * len(shape), pipeline_mode=pl.Buffered(1))

    return pl.pallas_call(
        functools.partial(_mix_in_kernel, tm=tm),
        grid=(t // tm,),
        in_specs=[row(D_MODEL), lay((1, D_MODEL)), lay((D_MODEL, IN_WIDTH)), lay((1, D_A)), lay((1, D_A)),
                  lay((A_GROUPS // 2, CHUNK, 2 * CHUNK)), lay((A_GROUPS // 2, CHUNK, LANES)),
                  _resident((B_HEAD, 2 * B_HEAD))],
        out_specs=[row(D_A), row(D_B), row(D_C), row(N_BRANCH * D_MODEL)],
        out_shape=[jax.ShapeDtypeStruct((t, D_A), BF16), jax.ShapeDtypeStruct((t, D_B), jnp.uint32),
                   jax.ShapeDtypeStruct((t, D_C), F32),
                   jax.ShapeDtypeStruct((t, N_BRANCH * D_MODEL), BF16)],
        compiler_params=_cparams("parallel"),
        name="mix_in",
    )(x, nm, w_in, lng, lnb, wsp, bsp, dftc)


FFT_SLABS = SUBLANES


def _packed_rows(p):
    re, im = _unpack_pair(p)
    return jnp.concatenate([re, im], axis=0).astype(BF16)


def _fft1_kernel(x_hbm, a_ref, twr_ref, twi_ref, y_hbm, inbuf, outbuf, sem_in, sem_out, *, n1, nj, n_steps):
    b = pl.program_id(0)
    jb = pl.program_id(1)
    step = b * nj + jb
    slot = step % 2

    def in_copy(bb, jj, j, s):
        return pltpu.make_async_copy(x_hbm.at[bb, :, jj * FFT_SLABS + j, :], inbuf.at[s, j], sem_in.at[s])

    def out_copy(bb, jj, j, s):
        return pltpu.make_async_copy(outbuf.at[s, j], y_hbm.at[bb, :, jj * FFT_SLABS + j, :], sem_out.at[s])

    @pl.when(step == 0)
    def _():
        for j in range(FFT_SLABS):
            in_copy(b, jb, j, 0).start()

    @pl.when(step + 1 < n_steps)
    def _():
        nxt = step + 1
        for j in range(FFT_SLABS):
            in_copy(nxt // nj, nxt % nj, j, 1 - slot).start()

    @pl.when(step >= 2)
    def _():
        for j in range(FFT_SLABS):
            out_copy(b, jb, j, slot).wait()

    for j in range(FFT_SLABS):
        in_copy(b, jb, j, slot).wait()
    for j in range(FFT_SLABS):
        y = _dot(a_ref[...], _packed_rows(inbuf[slot, j]))
        yr, yi = y[:n1], y[n1:]
        tr, ti = twr_ref[j], twi_ref[j]
        outbuf[slot, j] = _pack_pair(yr * tr - yi * ti, yr * ti + yi * tr)
        out_copy(b, jb, j, slot).start()

    @pl.when(step == n_steps - 1)
    def _():
        for j in range(FFT_SLABS):
            out_copy(b, jb, j, slot).wait()
        for j in range(FFT_SLABS):
            out_copy(b, jb, j, 1 - slot).wait()


def _fft1(xri, a_mat, twr, twi, bsz, n1):
    n2 = FFT_N2
    nj = n2 // FFT_SLABS
    x4 = xri.reshape(bsz, n1, n2, D_B)
    tw = pl.BlockSpec((FFT_SLABS, n1, 1), lambda b, j: (j, 0, 0))
    return pl.pallas_call(
        functools.partial(_fft1_kernel, n1=n1, nj=nj, n_steps=bsz * nj),
        grid=(bsz, nj),
        in_specs=[pl.BlockSpec(memory_space=pl.ANY), _resident((2 * n1, 2 * n1)), tw, tw],
        out_specs=pl.BlockSpec(memory_space=pl.ANY),
        out_shape=jax.ShapeDtypeStruct((bsz, n1, n2, D_B), jnp.uint32),
        scratch_shapes=[pltpu.VMEM((2, FFT_SLABS, n1, D_B), jnp.uint32), pltpu.VMEM((2, FFT_SLABS, n1, D_B), jnp.uint32),
                        pltpu.SemaphoreType.DMA((2,)), pltpu.SemaphoreType.DMA((2,))],
        compiler_params=_cparams("arbitrary", "arbitrary"),
        name="fft1",
    )(x4, a_mat, twr, twi)


def _fft2_kernel(y_ref, w_ref, o_hbm, outbuf, sem, *, nk, n_steps):
    b = pl.program_id(0)
    kb = pl.program_id(1)
    step = b * nk + kb
    slot = step % 2

    def out_copy(kk, s):
        return pltpu.make_async_copy(outbuf.at[s, kk], o_hbm.at[b, :, kb * FFT_SLABS + kk, :], sem.at[s])

    @pl.when(step >= 2)
    def _():
        for kk in range(FFT_SLABS):
            out_copy(kk, slot).wait()

    for kk in range(FFT_SLABS):
        outbuf[slot, kk] = _dot(w_ref[...], _packed_rows(y_ref[kk]))
        out_copy(kk, slot).start()

    @pl.when(step == n_steps - 1)
    def _():
        for kk in range(FFT_SLABS):
            out_copy(kk, slot).wait()
        for kk in range(FFT_SLABS):
            out_copy(kk, 1 - slot).wait()


def _fft2(ypk, w2, bsz, n1):
    n2 = FFT_N2
    nk = n1 // FFT_SLABS
    return pl.pallas_call(
        functools.partial(_fft2_kernel, nk=nk, n_steps=bsz * nk),
        grid=(bsz, nk),
        in_specs=[pl.BlockSpec((None, FFT_SLABS, n2, D_B), lambda b, k: (b, k, 0, 0)), _resident((n2, 2 * n2))],
        out_specs=pl.BlockSpec(memory_space=pl.ANY),
        out_shape=jax.ShapeDtypeStruct((bsz, n2, n1, D_B), F32),
        scratch_shapes=[pltpu.VMEM((2, FFT_SLABS, n2, D_B), F32), pltpu.SemaphoreType.DMA((2,))],
        compiler_params=_cparams("arbitrary", "arbitrary"),
        name="fft2",
    )(ypk, w2)


def _dft_constants(n1):
    n2 = FFT_N2
    k = np.arange(B_HEAD)
    ang = 2.0 * np.pi * np.outer(k, k) / B_HEAD
    dftc = np.concatenate([np.cos(ang), -np.sin(ang)], axis=1) / math.sqrt(B_HEAD)
    k1 = np.arange(n1)
    ang1 = 2.0 * np.pi * np.outer(k1, k1) / n1
    ar, ai = np.cos(ang1) / math.sqrt(n1), -np.sin(ang1) / math.sqrt(n1)
    a_mat = np.block([[ar, -ai], [ai, ar]])
    n2i = np.arange(n2)
    angt = 2.0 * np.pi * np.outer(n2i, k1) / (n1 * n2)
    twr, twi = np.cos(angt)[:, :, None], -np.sin(angt)[:, :, None]
    ang2 = 2.0 * np.pi * np.outer(n2i, n2i) / n2
    w2 = np.concatenate([np.cos(ang2), np.sin(ang2)], axis=1) / math.sqrt(n2)
    return (jnp.asarray(dftc, BF16), jnp.asarray(a_mat, BF16), jnp.asarray(twr, F32), jnp.asarray(twi, F32),
            jnp.asarray(w2, BF16))


CONV_ROWS = 32


def _conv_kernel(cur_ref, prev_ref, next_ref, w_ref, b_ref, g_ref, bb_ref, o_ref, sh_ref, *, tc, seq):
    i = pl.program_id(0)
    first = (i * tc) % seq == 0
    last = ((i + 1) * tc) % seq == 0
    n_ext = tc + 2 * CONV_HALO
    ext = jnp.concatenate([jnp.where(first, 0.0, prev_ref[...]), cur_ref[...],
                           jnp.where(last, 0.0, next_ref[...])], axis=0)
    sh_ref[0] = ext
    for s in range(1, SUBLANES):
        sh_ref[s] = pltpu.roll(ext, n_ext - s, axis=0)
    base = CONV_HALO - CONV_WIDTH // 2
    groups = CONV_ROWS // SUBLANES
    for r in range(tc // CONV_ROWS):
        acc = [jnp.zeros((SUBLANES, D_C), F32) for _ in range(groups)]
        for k in range(CONV_WIDTH):
            s = (base + k) % SUBLANES
            lo = r * CONV_ROWS + base + k - s
            wk = w_ref[k]
            for q in range(groups):
                acc[q] = acc[q] + wk * sh_ref[s, lo + q * SUBLANES:lo + (q + 1) * SUBLANES, :]
        y = _ln(jnp.concatenate(acc, axis=0) + b_ref[...], g_ref[...], bb_ref[...])
        o_ref[r * CONV_ROWS:(r + 1) * CONV_ROWS] = (y * _sigmoid(y)).astype(o_ref.dtype)


def _conv(layer, yglu, conv_w, conv_b, lng, lnb, seq, tc):
    t = yglu.shape[0]
    hb = tc // CONV_HALO
    n_halo = t // CONV_HALO

    def lay(shape):
        return pl.BlockSpec((None,) + shape, lambda i: (layer,) + (0,) * len(shape), pipeline_mode=pl.Buffered(1))

    return pl.pallas_call(
        functools.partial(_conv_kernel, tc=tc, seq=seq),
        grid=(t // tc,),
        in_specs=[pl.BlockSpec((tc, D_C), lambda i: (i, 0)),
                  pl.BlockSpec((CONV_HALO, D_C), lambda i: (jnp.maximum(i * hb - 1, 0), 0)),
                  pl.BlockSpec((CONV_HALO, D_C), lambda i: (jnp.minimum((i + 1) * hb, n_halo - 1), 0)),
                  lay((CONV_WIDTH, SUBLANES, D_C)), lay((1, D_C)), lay((1, D_C)), lay((1, D_C))],
        out_specs=pl.BlockSpec((tc, D_C), lambda i: (i, 0)),
        out_shape=jax.ShapeDtypeStruct((t, D_C), BF16),
        scratch_shapes=[pltpu.VMEM((SUBLANES, tc + 2 * CONV_HALO, D_C), F32)],
        compiler_params=_cparams("parallel"),
        name="conv",
    )(yglu, yglu, yglu, conv_w, conv_b, lng, lnb)


def _mix_out_kernel(x_ref, ya_ref, yb_ref, yc_ref, gt_ref, wa_ref, wb_ref, wc_ref, wo_ref, nf_ref,
                    wrh_ref, wrl_ref, br_ref, x1_ref, hn_ref, ri_ref, rw_ref, *, tm):
    m = gt_ref[:, 0:D_MODEL].astype(F32) * _dot(ya_ref[...], wa_ref[...])
    m = m + gt_ref[:, D_MODEL:2 * D_MODEL].astype(F32) * _dot(yb_ref[...].astype(BF16), wb_ref[...])
    m = m + gt_ref[:, 2 * D_MODEL:3 * D_MODEL].astype(F32) * _dot(yc_ref[...], wc_ref[...])
    x1 = x_ref[...] + _dot(m.astype(BF16), wo_ref[...])
    x1_ref[...] = x1
    hn = _rms(x1, nf_ref[...])
    hn_ref[...] = _pack_row_halves(hn)

    hn_hi = hn.astype(BF16)
    hn_lo = (hn - hn_hi.astype(F32)).astype(BF16)
    nt = (((1,), (1,)), ((), ()))
    lg = (lax.dot_general(wrh_ref[...], hn_hi, nt, preferred_element_type=F32)
          + lax.dot_general(wrh_ref[...], hn_lo, nt, preferred_element_type=F32)
          + lax.dot_general(wrl_ref[...], hn_hi, nt, preferred_element_type=F32)) + br_ref[...]
    gl = [lg[g:g + 1, :] for g in range(N_GROUPS)]
    best = gl[0]
    gsel = jnp.zeros((1, tm), jnp.int32)
    for g in range(1, N_GROUPS):
        better = gl[g] > best
        best = jnp.where(better, gl[g], best)
        gsel = jnp.where(better, g, gsel)
    denom = jnp.exp(gl[0] - best)
    for g in range(1, N_GROUPS):
        denom = denom + jnp.exp(gl[g] - best)
    pg = 1.0 / denom
    es = lg[EXPERT_ROW0:EXPERT_ROW0 + E_PER_GROUP, :]
    for g in range(1, N_GROUPS):
        lo = EXPERT_ROW0 + g * E_PER_GROUP
        es = jnp.where(gsel == g, lg[lo:lo + E_PER_GROUP, :], es)
    slot = lax.broadcasted_iota(jnp.int32, (E_PER_GROUP, tm), 0)
    v0 = jnp.max(es, axis=0, keepdims=True)
    i0 = jnp.min(jnp.where(es == v0, slot, E_PER_GROUP), axis=0, keepdims=True)
    es1 = jnp.where(slot == i0, -jnp.inf, es)
    v1 = jnp.max(es1, axis=0, keepdims=True)
    i1 = jnp.min(jnp.where(es1 == v1, slot, E_PER_GROUP), axis=0, keepdims=True)
    ex = jnp.exp(v1 - v0)
    p0 = 1.0 / (1.0 + ex)
    p1 = ex / (1.0 + ex)
    row = lax.broadcasted_iota(jnp.int32, (SUBLANES, tm), 0)
    e0 = gsel * E_PER_GROUP + i0
    e1 = gsel * E_PER_GROUP + i1
    ri_ref[...] = jnp.where(row == 0, e0, jnp.where(row == 1, e1, 0))
    wrow = lax.broadcasted_iota(jnp.int32, (LANES, tm), 0)
    rw_ref[...] = jnp.where(wrow == 0, pg * p0, jnp.where(wrow == 1, pg * p1, 0.0)).T


def _mix_out(layer, x, ya, yb, yc, gates, wa, wb, wc, wo, nf, wr_hi, wr_lo, br, tm):
    t = x.shape[0]

    def row(w):
        return pl.BlockSpec((tm, w), lambda i: (i, 0))

    def lay(shape):
        return pl.BlockSpec((None,) + shape, lambda i: (layer,) + (0,) * len(shape), pipeline_mode=pl.Buffered(1))

    return pl.pallas_call(
        functools.partial(_mix_out_kernel, tm=tm),
        grid=(t // tm,),
        in_specs=[row(D_MODEL), row(D_A), row(D_B), row(D_C), row(N_BRANCH * D_MODEL),
                  lay((D_A, D_MODEL)), lay((D_B, D_MODEL)), lay((D_C, D_MODEL)), lay((D_MODEL, D_MODEL)),
                  lay((1, D_MODEL)), lay((ROUTE_ROWS, D_MODEL)), lay((ROUTE_ROWS, D_MODEL)), lay((ROUTE_ROWS, 1))],
        out_specs=[row(D_MODEL), row(D_MODEL // 2), pl.BlockSpec((SUBLANES, tm), lambda i: (0, i)), row(LANES)],
        out_shape=[jax.ShapeDtypeStruct((t, D_MODEL), F32), jax.ShapeDtypeStruct((t, D_MODEL // 2), jnp.uint32),
                   jax.ShapeDtypeStruct((SUBLANES, t), jnp.int32), jax.ShapeDtypeStruct((t, LANES), F32)],
        compiler_params=_cparams("parallel"),
        name="mix_out",
    )(x, ya, yb, yc, gates, wa, wb, wc, wo, nf, wr_hi, wr_lo, br)


def _rank_kernel(e_ref, u_ref, rank_ref, cnt_ref, run_ref):
    k = pl.program_id(0)
    j = pl.program_id(1)

    @pl.when((k == 0) & (j == 0))
    def _():
        run_ref[...] = jnp.zeros_like(run_ref)

    tl = e_ref.shape[1]
    e = jnp.where(k == 0, e_ref[0:1, :], e_ref[1:2, :])
    onehot = lax.broadcasted_iota(jnp.int32, (N_EXPERTS, tl), 0) == e
    csum = _dot(jnp.where(onehot, 1.0, 0.0).astype(BF16), u_ref[...])
    base = run_ref[:, 0:1]
    rank = jnp.sum(jnp.where(onehot, csum - 1.0 + base, 0.0), axis=0, keepdims=True)
    rank_ref[...] = rank.astype(jnp.int32)
    run_ref[...] = run_ref[...] + csum[:, tl - 1:tl]
    cnt_ref[...] = run_ref[...]


def _route_plan(ri, t):
    tl = RANK_TILE if t % RANK_TILE == 0 else CHUNK
    upper = jnp.asarray(np.triu(np.ones((tl, tl), np.float32)), BF16)
    rank, cnt = pl.pallas_call(
        _rank_kernel,
        grid=(2, t // tl),
        in_specs=[pl.BlockSpec((SUBLANES, tl), lambda k, j: (0, j)), _resident((tl, tl))],
        out_specs=[pl.BlockSpec((None, 1, tl), lambda k, j: (k, 0, j)), _resident((N_EXPERTS, LANES))],
        out_shape=[jax.ShapeDtypeStruct((2, 1, t), jnp.int32), jax.ShapeDtypeStruct((N_EXPERTS, LANES), F32)],
        scratch_shapes=[pltpu.VMEM((N_EXPERTS, LANES), F32)],
        compiler_params=_cparams("arbitrary", "arbitrary"),
        name="rank",
    )(ri, upper)
    counts = cnt[:, 0].astype(jnp.int32)
    padded = (counts + MOE_BLOCK - 1) // MOE_BLOCK * MOE_BLOCK
    pad_end = jnp.cumsum(padded)
    pad_start = pad_end - padded
    experts = jnp.arange(N_EXPERTS, dtype=jnp.int32)
    dest = rank[:, 0, :] + jnp.sum(jnp.where(ri[:2, :, None] == experts, pad_start, 0), axis=-1)
    n_blocks = 2 * t // MOE_BLOCK + N_EXPERTS
    first_row = jnp.arange(n_blocks, dtype=jnp.int32) * MOE_BLOCK
    block_e = jnp.minimum(jnp.sum((pad_end[None, :] <= first_row[:, None]).astype(jnp.int32), axis=1), N_EXPERTS - 1)
    n_used = (pad_end[-1] // MOE_BLOCK).astype(jnp.int32).reshape(1)
    tail = pad_end[-1] + experts * MOE_BLOCK
    zero_blk = jnp.concatenate([jnp.where(padded > 0, pad_end - MOE_BLOCK, -1),
                                jnp.where(tail < n_blocks * MOE_BLOCK, tail, -1)]).astype(jnp.int32)
    return dest.astype(jnp.int32), block_e.astype(jnp.int32), n_used, zero_blk, n_blocks


DISPATCH_UNROLL = 8


def _dispatch_kernel(zb_ref, d_ref, hn_ref, xg_hbm, zero_ref, sem, *, td):
    i = pl.program_id(0)

    def zero_copy(e):
        start = pl.multiple_of(zb_ref[e], MOE_BLOCK)
        return pltpu.make_async_copy(zero_ref, xg_hbm.at[pl.ds(start, MOE_BLOCK)], sem.at[1])

    @pl.when(i == 0)
    def _():
        zero_ref[...] = jnp.zeros_like(zero_ref)
        for e in range(2 * N_EXPERTS):
            @pl.when(zb_ref[e] >= 0)
            def _():
                zero_copy(e).start()
        for e in range(2 * N_EXPERTS):
            @pl.when(zb_ref[e] >= 0)
            def _():
                zero_copy(e).wait()

    def body(c, carry):
        for u in range(DISPATCH_UNROLL):
            r = c * DISPATCH_UNROLL + u
            src = hn_ref.at[pl.ds(r, 1)]
            pltpu.make_async_copy(src, xg_hbm.at[pl.ds(d_ref[0, 0, r], 1)], sem.at[0]).start()
            pltpu.make_async_copy(src, xg_hbm.at[pl.ds(d_ref[0, 0, td + r], 1)], sem.at[0]).start()
        return carry

    lax.fori_loop(0, td // DISPATCH_UNROLL, body, 0)
    for _ in range(2):
        pltpu.make_async_copy(hn_ref, xg_hbm.at[pl.ds(0, td)], sem.at[0]).wait()


def _dest_tiles(dest, n_tiles, tile):
    return dest.reshape(2, n_tiles, tile).transpose(1, 0, 2).reshape(n_tiles, 1, 2 * tile)


def _dispatch(hn, dest, zero_blk, n_blocks, td):
    t = hn.shape[0]
    n_tiles = t // td
    grid_spec = pltpu.PrefetchScalarGridSpec(
        num_scalar_prefetch=1,
        grid=(n_tiles,),
        in_specs=[pl.BlockSpec((1, 1, 2 * td), lambda i, zb: (i, 0, 0), memory_space=pltpu.SMEM),
                  pl.BlockSpec((td, D_MODEL // 2), lambda i, zb: (i, 0))],
        out_specs=pl.BlockSpec(memory_space=pl.ANY),
        scratch_shapes=[pltpu.VMEM((MOE_BLOCK, D_MODEL // 2), jnp.uint32), pltpu.SemaphoreType.DMA((2,))],
    )
    return pl.pallas_call(
        functools.partial(_dispatch_kernel, td=td),
        grid_spec=grid_spec,
        out_shape=jax.ShapeDtypeStruct((n_blocks * MOE_BLOCK, D_MODEL // 2), jnp.uint32),
        compiler_params=_cparams("arbitrary"),
        name="dispatch",
    )(zero_blk, _dest_tiles(dest, n_tiles, td), hn)


def _expert_kernel(be_ref, nu_ref, x_ref, w1_ref, w3_ref, w2_ref, o_ref, w1b, w3b, w2b):
    i = pl.program_id(0)

    @pl.when((i == 0) | (be_ref[i] != be_ref[jnp.maximum(i - 1, 0)]))
    def _():
        w1b[...] = w1_ref[...].astype(BF16)
        w3b[...] = w3_ref[...].astype(BF16)
        w2b[...] = w2_ref[...].astype(BF16)

    @pl.when(i < nu_ref[0])
    def _():
        xb = _unpack_row_halves(x_ref[...]).astype(BF16)
        h1 = _dot(xb, w1b[...])
        h3 = _dot(xb, w3b[...])
        act = (h1 * _sigmoid(h1) * h3).astype(BF16)
        o_ref[...] = _pack_row_halves(_dot(act, w2b[...]))

    @pl.when(i >= nu_ref[0])
    def _():
        o_ref[...] = jnp.zeros_like(o_ref)


def _experts(layer, xg, block_e, n_used, n_blocks, w1, w3, w2):
    grid_spec = pltpu.PrefetchScalarGridSpec(
        num_scalar_prefetch=2,
        grid=(n_blocks,),
        in_specs=[
            pl.BlockSpec((MOE_BLOCK, D_MODEL // 2), lambda i, be, nu: (jnp.minimum(i, nu[0] - 1), 0)),
            pl.BlockSpec((None, None, D_MODEL, D_EXPERT), lambda i, be, nu: (layer, be[i], 0, 0)),
            pl.BlockSpec((None, None, D_MODEL, D_EXPERT), lambda i, be, nu: (layer, be[i], 0, 0)),
            pl.BlockSpec((None, None, D_EXPERT, D_MODEL), lambda i, be, nu: (layer, be[i], 0, 0)),
        ],
        out_specs=pl.BlockSpec((MOE_BLOCK, D_MODEL // 2), lambda i, be, nu: (i, 0)),
        scratch_shapes=[pltpu.VMEM((D_MODEL, D_EXPERT), BF16), pltpu.VMEM((D_MODEL, D_EXPERT), BF16),
                        pltpu.VMEM((D_EXPERT, D_MODEL), BF16)],
    )
    return pl.pallas_call(
        _expert_kernel,
        grid_spec=grid_spec,
        out_shape=jax.ShapeDtypeStruct((n_blocks * MOE_BLOCK, D_MODEL // 2), jnp.uint32),
        compiler_params=_cparams("arbitrary"),
        name="experts",
    )(block_e, n_used, xg, w1, w3, w2)


def _combine_kernel(d_ref, dn_ref, yg_hbm, x1_ref, rw_ref, p_ref, np_ref, wg_ref, wp_ref, nfin_ref, o_ref, gbuf, sem,
                    *, tq, n_tiles, final):
    i = pl.program_id(0)
    slot = i % 2

    def issue(dref, s):
        def body(c, carry):
            for u in range(DISPATCH_UNROLL):
                r = c * DISPATCH_UNROLL + u
                pltpu.make_async_copy(yg_hbm.at[pl.ds(dref[0, 0, r], 1)], gbuf.at[s, pl.ds(r, 1)],
                                      sem.at[s]).start()
            return carry
        lax.fori_loop(0, 2 * tq // DISPATCH_UNROLL, body, 0)

    def wait(s):
        pltpu.make_async_copy(yg_hbm.at[pl.ds(0, 2 * tq)], gbuf.at[s], sem.at[s]).wait()

    @pl.when(i == 0)
    def _():
        issue(d_ref, 0)

    issue(dn_ref, 1 - slot)
    wait(slot)
    moe = (rw_ref[:, 0:1] * _unpack_row_halves(gbuf[slot, 0:tq, :])
           + rw_ref[:, 1:2] * _unpack_row_halves(gbuf[slot, tq:2 * tq, :]))
    x2 = x1_ref[...] + moe
    g = _sigmoid(_dot(_rms(x2, np_ref[...]).astype(BF16), wg_ref[...]))
    x3 = x2 + g * _dot(p_ref[...].astype(BF16), wp_ref[...])
    if final:
        x3 = _rms(x3, nfin_ref[...])
    o_ref[...] = x3

    @pl.when(i == n_tiles - 1)
    def _():
        wait(1 - slot)


def _combine(layer, dest, yg, x1, rw, p, norm_ple, wg, wp, nfin, tq, final):
    t = x1.shape[0]
    n_tiles = t // tq
    d3 = _dest_tiles(dest, n_tiles, tq)

    def lay(shape):
        return pl.BlockSpec((None,) + shape, lambda i: (layer,) + (0,) * len(shape), pipeline_mode=pl.Buffered(1))

    return pl.pallas_call(
        functools.partial(_combine_kernel, tq=tq, n_tiles=n_tiles, final=final),
        grid=(n_tiles,),
        in_specs=[
            pl.BlockSpec((1, 1, 2 * tq), lambda i: (i, 0, 0), memory_space=pltpu.SMEM),
            pl.BlockSpec((1, 1, 2 * tq), lambda i: (jnp.minimum(i + 1, n_tiles - 1), 0, 0), memory_space=pltpu.SMEM),
            pl.BlockSpec(memory_space=pl.ANY),
            pl.BlockSpec((tq, D_MODEL), lambda i: (i, 0)),
            pl.BlockSpec((tq, LANES), lambda i: (i, 0)),
            pl.BlockSpec((None, tq, PLE_DIM), lambda i: (layer, i, 0)),
            lay((1, D_MODEL)), lay((D_MODEL, D_MODEL)), lay((PLE_DIM, D_MODEL)), _resident((1, D_MODEL)),
        ],
        out_specs=pl.BlockSpec((tq, D_MODEL), lambda i: (i, 0)),
        out_shape=jax.ShapeDtypeStruct((t, D_MODEL), F32),
        scratch_shapes=[pltpu.VMEM((2, 2 * tq, D_MODEL // 2), jnp.uint32), pltpu.SemaphoreType.DMA((2,))],
        compiler_params=_cparams("arbitrary"),
        name="combine",
    )(d3, d3, yg, x1, rw, p, norm_ple, wg, wp, nfin)


def _tile(t, want):
    return want if t % want == 0 else CHUNK


def _run_group(x, p, wts):
    bsz, seq, _ = x.shape
    depth = p.shape[0]
    t = bsz * seq
    n1 = seq // FFT_N2
    tm = _tile(t, 512)
    tq = _tile(t, 256)
    dftc, a_mat, twr, twi, w2f = _dft_constants(n1)
    xf = x.reshape(t, D_MODEL)
    pf = p.reshape(depth, t, PLE_DIM)
    for l in range(depth):
        ya, xri, yglu, gates = _mix_in(l, xf, wts['norm_mix'], wts['w_in'], wts['gmlp_ln_g'], wts['gmlp_ln_b'],
                                          wts['wsp'], wts['bsp'], dftc, tm)
        ypk = _fft1(xri, a_mat, twr, twi, bsz, n1)
        yb = _fft2(ypk, w2f, bsz, n1).reshape(t, D_B)
        yc = _conv(l, yglu, wts['conv_w'], wts['conv_b'], wts['conv_ln_g'], wts['conv_ln_b'], seq, tq)
        x1, hn, ri, rw = _mix_out(l, xf, ya, yb, yc, gates, wts['w_out_a'], wts['w_out_b'], wts['w_out_c'],
                                  wts['w_o'], wts['norm_ffn'], wts['wr_hi'], wts['wr_lo'], wts['br'], tm)
        dest, block_e, n_used, zero_blk, n_blocks = _route_plan(ri, t)
        xg = _dispatch(hn, dest, zero_blk, n_blocks, tm)
        yg = _experts(l, xg, block_e, n_used, n_blocks, wts['w1'], wts['w3'], wts['w2'])
        xf = _combine(l, dest, yg, x1, rw, pf, wts['norm_ple'], wts['w_ple_gate'], wts['w_ple_proj'],
                      wts['norm_final'], tq, l == depth - 1)
    return xf.reshape(bsz, seq, D_MODEL)


def _prepare_weights(norm_mix, w_in, gmlp_ln_g, gmlp_ln_b, w_spatial, b_spatial, conv_w, conv_b, conv_ln_g,
                     conv_ln_b, w_out_a, w_out_b, w_out_c, w_o, norm_ffn, w_router_group, b_router_group,
                     w_router_expert, b_router_expert, w1, w3, w2, norm_ple, w_ple_gate, w_ple_proj, norm_final):
    depth = w_in.shape[0]
    row = lambda a: a.reshape(depth, 1, -1)
    wsp = jnp.concatenate([w_spatial[:, 0::2], w_spatial[:, 1::2]], axis=-1).astype(BF16)
    bpair = jnp.stack([b_spatial[:, 0::2], b_spatial[:, 1::2]], axis=-1)
    bsp = jnp.repeat(bpair, A_HEAD, axis=-1)
    conv_wp = jnp.broadcast_to(conv_w[:, :, None, :], (depth, CONV_WIDTH, SUBLANES, D_C))
    pad_g = jnp.zeros((depth, EXPERT_ROW0 - N_GROUPS, D_MODEL), F32)
    pad_e = jnp.zeros((depth, ROUTE_ROWS - EXPERT_ROW0 - N_EXPERTS, D_MODEL), F32)
    wr = jnp.concatenate([jnp.swapaxes(w_router_group, 1, 2), pad_g, jnp.swapaxes(w_router_expert, 1, 2), pad_e],
                         axis=1)
    br = jnp.concatenate([b_router_group, pad_g[:, :, 0], b_router_expert, pad_e[:, :, 0]], axis=1)[:, :, None]
    wr_hi = wr.astype(BF16)
    wr_lo = (wr - wr_hi.astype(F32)).astype(BF16)
    return dict(
        norm_mix=row(norm_mix), w_in=w_in.astype(BF16), gmlp_ln_g=row(gmlp_ln_g), gmlp_ln_b=row(gmlp_ln_b),
        wsp=wsp, bsp=bsp, conv_w=conv_wp, conv_b=row(conv_b), conv_ln_g=row(conv_ln_g), conv_ln_b=row(conv_ln_b),
        w_out_a=w_out_a.astype(BF16), w_out_b=w_out_b.astype(BF16), w_out_c=w_out_c.astype(BF16),
        w_o=w_o.astype(BF16), norm_ffn=row(norm_ffn), wr_hi=wr_hi, wr_lo=wr_lo, br=br,
        w1=w1, w3=w3, w2=w2,
        norm_ple=row(norm_ple), w_ple_gate=w_ple_gate.astype(BF16), w_ple_proj=w_ple_proj.astype(BF16),
        norm_final=norm_final.reshape(1, D_MODEL))


def kernel(x_prompt, x_sample, p_prompt, p_sample, norm_mix, w_in, gmlp_ln_g, gmlp_ln_b, w_spatial, b_spatial, conv_w, conv_b, conv_ln_g, conv_ln_b, w_out_a, w_out_b, w_out_c, w_o, norm_ffn, w_router_group, b_router_group, w_router_expert, b_router_expert, w1, w3, w2, norm_ple, w_ple_gate, w_ple_proj, norm_final):
    wts = _prepare_weights(norm_mix, w_in, gmlp_ln_g, gmlp_ln_b, w_spatial, b_spatial, conv_w, conv_b, conv_ln_g,
                           conv_ln_b, w_out_a, w_out_b, w_out_c, w_o, norm_ffn, w_router_group, b_router_group,
                           w_router_expert, b_router_expert, w1, w3, w2, norm_ple, w_ple_gate, w_ple_proj,
                           norm_final)
    return (_run_group(x_prompt, p_prompt, wts), _run_group(x_sample, p_sample, wts))
```

```python
import functools
import math

import numpy as np
import jax
import jax.numpy as jnp
from jax import lax
from jax.experimental import pallas as pl
from jax.experimental.pallas import tpu as pltpu

D_MODEL = 1024
CHUNK = 128
A_GROUPS = 8
A_HEAD = 64
D_A = A_GROUPS * A_HEAD
B_GROUPS = 4
B_HEAD = 128
D_B = B_GROUPS * B_HEAD
D_C = 512
CONV_WIDTH = 31
CONV_HALO = 16
N_BRANCH = 3
O1 = 2 * D_A
O2 = O1 + D_B
O3 = O2 + 2 * D_C
IN_WIDTH = O3 + N_BRANCH * D_MODEL
N_GROUPS = 4
E_PER_GROUP = 8
N_EXPERTS = N_GROUPS * E_PER_GROUP
D_EXPERT = 512
MOE_BLOCK = 256
PLE_DIM = 256
EPS = 1e-6

LANES = 128
SUBLANES = 8
FFT_N2 = 128
ROUTE_ROWS = 128
EXPERT_ROW0 = 8
RANK_TILE = 1024
VMEM_LIMIT = 56 * 1024 * 1024

F32 = jnp.float32
BF16 = jnp.bfloat16


def _cparams(*sem):
    return pltpu.CompilerParams(dimension_semantics=sem, vmem_limit_bytes=VMEM_LIMIT)


def _resident(shape, index=None):
    idx = tuple(index) if index is not None else (0,) * len(shape)
    return pl.BlockSpec(shape, lambda *_: idx, pipeline_mode=pl.Buffered(1))


def _rms(x, g):
    return x * lax.rsqrt(jnp.mean(x * x, axis=-1, keepdims=True) + EPS) * g


def _ln(x, g, b):
    mu = jnp.mean(x, axis=-1, keepdims=True)
    xc = x - mu
    return xc * lax.rsqrt(jnp.mean(xc * xc, axis=-1, keepdims=True) + EPS) * g + b


def _sigmoid(x):
    return 1.0 / (1.0 + jnp.exp(-x))


def _dot(a, b):
    return jnp.dot(a, b, preferred_element_type=F32)


def _pack_pair(a, b):
    hi = lax.bitcast_convert_type(a.astype(BF16).astype(F32), jnp.uint32)
    lo = lax.bitcast_convert_type(b.astype(BF16).astype(F32), jnp.uint32)
    return hi | (lo >> 16)


def _unpack_pair(p):
    a = lax.bitcast_convert_type(p & jnp.uint32(0xFFFF0000), F32)
    b = lax.bitcast_convert_type(p << 16, F32)
    return a, b


def _pack_row_halves(x):
    n = x.shape[1] // 2
    return _pack_pair(x[:, :n], x[:, n:])


def _unpack_row_halves(p):
    return jnp.concatenate(_unpack_pair(p), axis=1)


def _mix_in_compute(x, nm_ref, w_ref, lng_ref, lnb_ref, wsp_ref, bsp_ref, dft_ref,
                    ya_ref, xri_ref, yglu_ref, gates_ref, tm):
    hb = _rms(x, nm_ref[...]).astype(BF16)

    def proj(lo, hi):
        return _dot(hb, w_ref[:, lo:hi])

    za = proj(0, O1)
    za = 0.5 * za * (1.0 + lax.erf(za * (1.0 / math.sqrt(2.0))))
    u = za[:, :D_A]
    v = _ln(za[:, D_A:], lng_ref[...], lnb_ref[...])
    n_chunks = tm // CHUNK
    low_half = lax.broadcasted_iota(jnp.int32, (CHUNK, LANES), 1) < A_HEAD
    for j in range(A_GROUPS // 2):
        cols = slice(j * LANES, (j + 1) * LANES)
        pieces = []
        for c in range(n_chunks):
            vp = v[c * CHUNK:(c + 1) * CHUNK, cols]
            pieces.append(jnp.concatenate([jnp.where(low_half, vp, 0.0), jnp.where(low_half, 0.0, vp)], axis=0))
        rhs = jnp.concatenate(pieces, axis=1).astype(BF16)
        sv = _dot(wsp_ref[j], rhs)
        for c in range(n_chunks):
            rows = slice(c * CHUNK, (c + 1) * CHUNK)
            svc = sv[:, c * LANES:(c + 1) * LANES] + bsp_ref[j]
            ya_ref[rows, cols] = (u[rows, cols] * svc).astype(ya_ref.dtype)

    zb = proj(O1, O2).astype(BF16)
    for g in range(B_GROUPS):
        cols = slice(g * B_HEAD, (g + 1) * B_HEAD)
        c = _dot(zb[:, cols], dft_ref[...])
        xri_ref[:, cols] = _pack_pair(c[:, :B_HEAD], c[:, B_HEAD:])

    zc = proj(O2, O3)
    yglu_ref[...] = zc[:, :D_C] * _sigmoid(zc[:, D_C:])

    for b in range(N_BRANCH):
        cols = slice(b * D_MODEL, (b + 1) * D_MODEL)
        gates_ref[:, cols] = _sigmoid(proj(O3 + b * D_MODEL, O3 + (b + 1) * D_MODEL)).astype(gates_ref.dtype)


def _mix_in_kernel(x_ref, nm_ref, w_ref, lng_ref, lnb_ref, wsp_ref, bsp_ref, dft_ref,
                   ya_ref, xri_ref, yglu_ref, gates_ref, *, tm):
    _mix_in_compute(x_ref[...], nm_ref, w_ref, lng_ref, lnb_ref, wsp_ref, bsp_ref, dft_ref,
                    ya_ref, xri_ref, yglu_ref, gates_ref, tm)


def _layer_spec(layer, shape):
    return pl.BlockSpec((None,) + shape, lambda i: (layer,) + (0,) * len(shape), pipeline_mode=pl.Buffered(1))


def _mix_in_weight_specs(layer):
    return [_layer_spec(layer, (1, D_MODEL)), _layer_spec(layer, (D_MODEL, IN_WIDTH)), _layer_spec(layer, (1, D_A)),
            _layer_spec(layer, (1, D_A)), _layer_spec(layer, (A_GROUPS // 2, CHUNK, 2 * CHUNK)),
            _layer_spec(layer, (A_GROUPS // 2, CHUNK, LANES)), _resident((B_HEAD, 2 * B_HEAD))]


def _mix_in_out(t, tm):
    def row(w):
        return pl.BlockSpec((tm, w), lambda i: (i, 0))
    specs = [row(D_A), row(D_B), row(D_C), row(N_BRANCH * D_MODEL)]
    shapes = [jax.ShapeDtypeStruct((t, D_A), BF16), jax.ShapeDtypeStruct((t, D_B), jnp.uint32),
              jax.ShapeDtypeStruct((t, D_C), F32), jax.ShapeDtypeStruct((t, N_BRANCH * D_MODEL), BF16)]
    return specs, shapes


def _mix_in(layer, x, nm, w_in, lng, lnb, wsp, bsp, dftc, tm):
    t = x.shape[0]
    out_specs, out_shape = _mix_in_out(t, tm)
    return pl.pallas_call(
        functools.partial(_mix_in_kernel, tm=tm),
        grid=(t // tm,),
        in_specs=[pl.BlockSpec((tm, D_MODEL), lambda i: (i, 0))] + _mix_in_weight_specs(layer),
        out_specs=out_specs,
        out_shape=out_shape,
        compiler_params=_cparams("parallel"),
        name="mix_in",
    )(x, nm, w_in, lng, lnb, wsp, bsp, dftc)


FFT_SLABS = SUBLANES


def _packed_rows(p):
    re, im = _unpack_pair(p)
    return jnp.concatenate([re, im], axis=0).astype(BF16)


def _fft1_kernel(x_hbm, a_ref, twr_ref, twi_ref, y_hbm, inbuf, outbuf, sem_in, sem_out, *, n1, nj, n_steps):
    b = pl.program_id(0)
    jb = pl.program_id(1)
    step = b * nj + jb
    slot = step % 2

    def in_copy(bb, jj, j, s):
        return pltpu.make_async_copy(x_hbm.at[bb, :, jj * FFT_SLABS + j, :], inbuf.at[s, j], sem_in.at[s])

    def out_copy(bb, jj, j, s):
        return pltpu.make_async_copy(outbuf.at[s, j], y_hbm.at[bb, :, jj * FFT_SLABS + j, :], sem_out.at[s])

    @pl.when(step == 0)
    def _():
        for j in range(FFT_SLABS):
            in_copy(b, jb, j, 0).start()

    @pl.when(step + 1 < n_steps)
    def _():
        nxt = step + 1
        for j in range(FFT_SLABS):
            in_copy(nxt // nj, nxt % nj, j, 1 - slot).start()

    @pl.when(step >= 2)
    def _():
        for j in range(FFT_SLABS):
            out_copy(b, jb, j, slot).wait()

    for j in range(FFT_SLABS):
        in_copy(b, jb, j, slot).wait()
    for j in range(FFT_SLABS):
        y = _dot(a_ref[...], _packed_rows(inbuf[slot, j]))
        yr, yi = y[:n1], y[n1:]
        tr, ti = twr_ref[j], twi_ref[j]
        outbuf[slot, j] = _pack_pair(yr * tr - yi * ti, yr * ti + yi * tr)
        out_copy(b, jb, j, slot).start()

    @pl.when(step == n_steps - 1)
    def _():
        for j in range(FFT_SLABS):
            out_copy(b, jb, j, slot).wait()
        for j in range(FFT_SLABS):
            out_copy(b, jb, j, 1 - slot).wait()


def _fft1(xri, a_mat, twr, twi, bsz, n1):
    n2 = FFT_N2
    nj = n2 // FFT_SLABS
    x4 = xri.reshape(bsz, n1, n2, D_B)
    tw = pl.BlockSpec((FFT_SLABS, n1, 1), lambda b, j: (j, 0, 0))
    return pl.pallas_call(
        functools.partial(_fft1_kernel, n1=n1, nj=nj, n_steps=bsz * nj),
        grid=(bsz, nj),
        in_specs=[pl.BlockSpec(memory_space=pl.ANY), _resident((2 * n1, 2 * n1)), tw, tw],
        out_specs=pl.BlockSpec(memory_space=pl.ANY),
        out_shape=jax.ShapeDtypeStruct((bsz, n1, n2, D_B), jnp.uint32),
        scratch_shapes=[pltpu.VMEM((2, FFT_SLABS, n1, D_B), jnp.uint32), pltpu.VMEM((2, FFT_SLABS, n1, D_B), jnp.uint32),
                        pltpu.SemaphoreType.DMA((2,)), pltpu.SemaphoreType.DMA((2,))],
        compiler_params=_cparams("arbitrary", "arbitrary"),
        name="fft1",
    )(x4, a_mat, twr, twi)


def _fft2_kernel(y_ref, w_ref, o_hbm, outbuf, sem, *, nk, n_steps):
    b = pl.program_id(0)
    kb = pl.program_id(1)
    step = b * nk + kb
    slot = step % 2

    def out_copy(kk, s):
        return pltpu.make_async_copy(outbuf.at[s, kk], o_hbm.at[b, :, kb * FFT_SLABS + kk, :], sem.at[s])

    @pl.when(step >= 2)
    def _():
        for kk in range(FFT_SLABS):
            out_copy(kk, slot).wait()

    for kk in range(FFT_SLABS):
        outbuf[slot, kk] = _dot(w_ref[...], _packed_rows(y_ref[kk]))
        out_copy(kk, slot).start()

    @pl.when(step == n_steps - 1)
    def _():
        for kk in range(FFT_SLABS):
            out_copy(kk, slot).wait()
        for kk in range(FFT_SLABS):
            out_copy(kk, 1 - slot).wait()


def _fft2(ypk, w2, bsz, n1):
    n2 = FFT_N2
    nk = n1 // FFT_SLABS
    return pl.pallas_call(
        functools.partial(_fft2_kernel, nk=nk, n_steps=bsz * nk),
        grid=(bsz, nk),
        in_specs=[pl.BlockSpec((None, FFT_SLABS, n2, D_B), lambda b, k: (b, k, 0, 0)), _resident((n2, 2 * n2))],
        out_specs=pl.BlockSpec(memory_space=pl.ANY),
        out_shape=jax.ShapeDtypeStruct((bsz, n2, n1, D_B), F32),
        scratch_shapes=[pltpu.VMEM((2, FFT_SLABS, n2, D_B), F32), pltpu.SemaphoreType.DMA((2,))],
        compiler_params=_cparams("arbitrary", "arbitrary"),
        name="fft2",
    )(ypk, w2)


def _dft_constants(n1):
    n2 = FFT_N2
    k = np.arange(B_HEAD)
    ang = 2.0 * np.pi * np.outer(k, k) / B_HEAD
    dftc = np.concatenate([np.cos(ang), -np.sin(ang)], axis=1) / math.sqrt(B_HEAD)
    k1 = np.arange(n1)
    ang1 = 2.0 * np.pi * np.outer(k1, k1) / n1
    ar, ai = np.cos(ang1) / math.sqrt(n1), -np.sin(ang1) / math.sqrt(n1)
    a_mat = np.block([[ar, -ai], [ai, ar]])
    n2i = np.arange(n2)
    angt = 2.0 * np.pi * np.outer(n2i, k1) / (n1 * n2)
    twr, twi = np.cos(angt)[:, :, None], -np.sin(angt)[:, :, None]
    ang2 = 2.0 * np.pi * np.outer(n2i, n2i) / n2
    w2 = np.concatenate([np.cos(ang2), np.sin(ang2)], axis=1) / math.sqrt(n2)
    return (jnp.asarray(dftc, BF16), jnp.asarray(a_mat, BF16), jnp.asarray(twr, F32), jnp.asarray(twi, F32),
            jnp.asarray(w2, BF16))


CONV_ROWS = 32


def _conv_compute(i, cur_ref, prev_ref, next_ref, w_ref, b_ref, g_ref, bb_ref, o_ref, sh_ref, tc, seq):
    first = (i * tc) % seq == 0
    last = ((i + 1) * tc) % seq == 0
    n_ext = tc + 2 * CONV_HALO
    ext = jnp.concatenate([jnp.where(first, 0.0, prev_ref[...]), cur_ref[...],
                           jnp.where(last, 0.0, next_ref[...])], axis=0)
    sh_ref[0] = ext
    for s in range(1, SUBLANES):
        sh_ref[s] = pltpu.roll(ext, n_ext - s, axis=0)
    base = CONV_HALO - CONV_WIDTH // 2
    groups = CONV_ROWS // SUBLANES
    for r in range(tc // CONV_ROWS):
        acc = [jnp.zeros((SUBLANES, D_C), F32) for _ in range(groups)]
        for k in range(CONV_WIDTH):
            s = (base + k) % SUBLANES
            lo = r * CONV_ROWS + base + k - s
            wk = w_ref[k]
            for q in range(groups):
                acc[q] = acc[q] + wk * sh_ref[s, lo + q * SUBLANES:lo + (q + 1) * SUBLANES, :]
        y = _ln(jnp.concatenate(acc, axis=0) + b_ref[...], g_ref[...], bb_ref[...])
        o_ref[r * CONV_ROWS:(r + 1) * CONV_ROWS] = (y * _sigmoid(y)).astype(o_ref.dtype)


def _mix_out_kernel(x_ref, ya_ref, yb_ref, glu_ref, glu_prev_ref, glu_next_ref, gt_ref,
                    cw_ref, cb_ref, cg_ref, cbb_ref, wa_ref, wb_ref, wc_ref, wo_ref, nf_ref,
                    wrh_ref, wrl_ref, br_ref, x1_ref, hn_ref, ri_ref, rw_ref, yc_ref, sh_ref, *, tm, seq):
    _conv_compute(pl.program_id(0), glu_ref, glu_prev_ref, glu_next_ref, cw_ref, cb_ref, cg_ref, cbb_ref,
                  yc_ref, sh_ref, tm, seq)
    m = gt_ref[:, 0:D_MODEL].astype(F32) * _dot(ya_ref[...], wa_ref[...])
    m = m + gt_ref[:, D_MODEL:2 * D_MODEL].astype(F32) * _dot(yb_ref[...].astype(BF16), wb_ref[...])
    m = m + gt_ref[:, 2 * D_MODEL:3 * D_MODEL].astype(F32) * _dot(yc_ref[...], wc_ref[...])
    x1 = x_ref[...] + _dot(m.astype(BF16), wo_ref[...])
    x1_ref[...] = x1
    hn = _rms(x1, nf_ref[...])
    hn_ref[...] = _pack_row_halves(hn)

    hn_hi = hn.astype(BF16)
    hn_lo = (hn - hn_hi.astype(F32)).astype(BF16)
    nt = (((1,), (1,)), ((), ()))
    lg = (lax.dot_general(wrh_ref[...], hn_hi, nt, preferred_element_type=F32)
          + lax.dot_general(wrh_ref[...], hn_lo, nt, preferred_element_type=F32)
          + lax.dot_general(wrl_ref[...], hn_hi, nt, preferred_element_type=F32)) + br_ref[...]
    gl = [lg[g:g + 1, :] for g in range(N_GROUPS)]
    best = gl[0]
    gsel = jnp.zeros((1, tm), jnp.int32)
    for g in range(1, N_GROUPS):
        better = gl[g] > best
        best = jnp.where(better, gl[g], best)
        gsel = jnp.where(better, g, gsel)
    denom = jnp.exp(gl[0] - best)
    for g in range(1, N_GROUPS):
        denom = denom + jnp.exp(gl[g] - best)
    pg = 1.0 / denom
    es = lg[EXPERT_ROW0:EXPERT_ROW0 + E_PER_GROUP, :]
    for g in range(1, N_GROUPS):
        lo = EXPERT_ROW0 + g * E_PER_GROUP
        es = jnp.where(gsel == g, lg[lo:lo + E_PER_GROUP, :], es)
    slot = lax.broadcasted_iota(jnp.int32, (E_PER_GROUP, tm), 0)
    v0 = jnp.max(es, axis=0, keepdims=True)
    i0 = jnp.min(jnp.where(es == v0, slot, E_PER_GROUP), axis=0, keepdims=True)
    es1 = jnp.where(slot == i0, -jnp.inf, es)
    v1 = jnp.max(es1, axis=0, keepdims=True)
    i1 = jnp.min(jnp.where(es1 == v1, slot, E_PER_GROUP), axis=0, keepdims=True)
    ex = jnp.exp(v1 - v0)
    p0 = 1.0 / (1.0 + ex)
    p1 = ex / (1.0 + ex)
    row = lax.broadcasted_iota(jnp.int32, (SUBLANES, tm), 0)
    e0 = gsel * E_PER_GROUP + i0
    e1 = gsel * E_PER_GROUP + i1
    ri_ref[...] = jnp.where(row == 0, e0, jnp.where(row == 1, e1, 0))
    wrow = lax.broadcasted_iota(jnp.int32, (LANES, tm), 0)
    rw_ref[...] = jnp.where(wrow == 0, pg * p0, jnp.where(wrow == 1, pg * p1, 0.0)).T


def _mix_out(layer, x, ya, yb, yglu, gates, conv_w, conv_b, conv_lng, conv_lnb, wa, wb, wc, wo, nf,
             wr_hi, wr_lo, br, seq, tm):
    t = x.shape[0]
    hb = tm // CONV_HALO
    n_halo = t // CONV_HALO

    def row(w):
        return pl.BlockSpec((tm, w), lambda i: (i, 0))

    def lay(shape):
        return _layer_spec(layer, shape)

    return pl.pallas_call(
        functools.partial(_mix_out_kernel, tm=tm, seq=seq),
        grid=(t // tm,),
        in_specs=[row(D_MODEL), row(D_A), row(D_B), row(D_C),
                  pl.BlockSpec((CONV_HALO, D_C), lambda i: (jnp.maximum(i * hb - 1, 0), 0)),
                  pl.BlockSpec((CONV_HALO, D_C), lambda i: (jnp.minimum((i + 1) * hb, n_halo - 1), 0)),
                  row(N_BRANCH * D_MODEL),
                  lay((CONV_WIDTH, SUBLANES, D_C)), lay((1, D_C)), lay((1, D_C)), lay((1, D_C)),
                  lay((D_A, D_MODEL)), lay((D_B, D_MODEL)), lay((D_C, D_MODEL)), lay((D_MODEL, D_MODEL)),
                  lay((1, D_MODEL)), lay((ROUTE_ROWS, D_MODEL)), lay((ROUTE_ROWS, D_MODEL)), lay((ROUTE_ROWS, 1))],
        out_specs=[row(D_MODEL), row(D_MODEL // 2), pl.BlockSpec((SUBLANES, tm), lambda i: (0, i)), row(LANES)],
        out_shape=[jax.ShapeDtypeStruct((t, D_MODEL), F32), jax.ShapeDtypeStruct((t, D_MODEL // 2), jnp.uint32),
                   jax.ShapeDtypeStruct((SUBLANES, t), jnp.int32), jax.ShapeDtypeStruct((t, LANES), F32)],
        scratch_shapes=[pltpu.VMEM((tm, D_C), BF16), pltpu.VMEM((SUBLANES, tm + 2 * CONV_HALO, D_C), F32)],
        compiler_params=_cparams("parallel"),
        name="mix_out",
    )(x, ya, yb, yglu, yglu, yglu, gates, conv_w, conv_b, conv_lng, conv_lnb, wa, wb, wc, wo, nf, wr_hi, wr_lo, br)


def _rank_kernel(e_ref, u_ref, rank_ref, cnt_ref, run_ref):
    k = pl.program_id(0)
    j = pl.program_id(1)

    @pl.when((k == 0) & (j == 0))
    def _():
        run_ref[...] = jnp.zeros_like(run_ref)

    tl = e_ref.shape[1]
    e = jnp.where(k == 0, e_ref[0:1, :], e_ref[1:2, :])
    onehot = lax.broadcasted_iota(jnp.int32, (N_EXPERTS, tl), 0) == e
    csum = _dot(jnp.where(onehot, 1.0, 0.0).astype(BF16), u_ref[...])
    base = run_ref[:, 0:1]
    rank = jnp.sum(jnp.where(onehot, csum - 1.0 + base, 0.0), axis=0, keepdims=True)
    rank_ref[...] = rank.astype(jnp.int32)
    run_ref[...] = run_ref[...] + csum[:, tl - 1:tl]
    cnt_ref[...] = run_ref[...]


def _route_plan(ri, t):
    tl = RANK_TILE if t % RANK_TILE == 0 else CHUNK
    upper = jnp.asarray(np.triu(np.ones((tl, tl), np.float32)), BF16)
    rank, cnt = pl.pallas_call(
        _rank_kernel,
        grid=(2, t // tl),
        in_specs=[pl.BlockSpec((SUBLANES, tl), lambda k, j: (0, j)), _resident((tl, tl))],
        out_specs=[pl.BlockSpec((None, 1, tl), lambda k, j: (k, 0, j)), _resident((N_EXPERTS, LANES))],
        out_shape=[jax.ShapeDtypeStruct((2, 1, t), jnp.int32), jax.ShapeDtypeStruct((N_EXPERTS, LANES), F32)],
        scratch_shapes=[pltpu.VMEM((N_EXPERTS, LANES), F32)],
        compiler_params=_cparams("arbitrary", "arbitrary"),
        name="rank",
    )(ri, upper)
    counts = cnt[:, 0].astype(jnp.int32)
    padded = (counts + MOE_BLOCK - 1) // MOE_BLOCK * MOE_BLOCK
    pad_end = jnp.cumsum(padded)
    pad_start = pad_end - padded
    experts = jnp.arange(N_EXPERTS, dtype=jnp.int32)
    dest = rank[:, 0, :] + jnp.sum(jnp.where(ri[:2, :, None] == experts, pad_start, 0), axis=-1)
    n_blocks = 2 * t // MOE_BLOCK + N_EXPERTS
    first_row = jnp.arange(n_blocks, dtype=jnp.int32) * MOE_BLOCK
    block_e = jnp.minimum(jnp.sum((pad_end[None, :] <= first_row[:, None]).astype(jnp.int32), axis=1), N_EXPERTS - 1)
    n_used = (pad_end[-1] // MOE_BLOCK).astype(jnp.int32).reshape(1)
    tail = pad_end[-1] + experts * MOE_BLOCK
    zero_blk = jnp.concatenate([jnp.where(padded > 0, pad_end - MOE_BLOCK, -1),
                                jnp.where(tail < n_blocks * MOE_BLOCK, tail, -1)]).astype(jnp.int32)
    return dest.astype(jnp.int32), block_e.astype(jnp.int32), n_used, zero_blk, n_blocks


DISPATCH_UNROLL = 8


def _dispatch_kernel(zb_ref, d_ref, hn_ref, xg_hbm, zero_ref, sem, *, td):
    i = pl.program_id(0)

    def zero_copy(e):
        start = pl.multiple_of(zb_ref[e], MOE_BLOCK)
        return pltpu.make_async_copy(zero_ref, xg_hbm.at[pl.ds(start, MOE_BLOCK)], sem.at[1])

    @pl.when(i == 0)
    def _():
        zero_ref[...] = jnp.zeros_like(zero_ref)
        for e in range(2 * N_EXPERTS):
            @pl.when(zb_ref[e] >= 0)
            def _():
                zero_copy(e).start()
        for e in range(2 * N_EXPERTS):
            @pl.when(zb_ref[e] >= 0)
            def _():
                zero_copy(e).wait()

    def body(c, carry):
        for u in range(DISPATCH_UNROLL):
            r = c * DISPATCH_UNROLL + u
            src = hn_ref.at[pl.ds(r, 1)]
            pltpu.make_async_copy(src, xg_hbm.at[pl.ds(d_ref[0, 0, r], 1)], sem.at[0]).start()
            pltpu.make_async_copy(src, xg_hbm.at[pl.ds(d_ref[0, 0, td + r], 1)], sem.at[0]).start()
        return carry

    lax.fori_loop(0, td // DISPATCH_UNROLL, body, 0)
    for _ in range(2):
        pltpu.make_async_copy(hn_ref, xg_hbm.at[pl.ds(0, td)], sem.at[0]).wait()


def _dest_tiles(dest, n_tiles, tile):
    return dest.reshape(2, n_tiles, tile).transpose(1, 0, 2).reshape(n_tiles, 1, 2 * tile)


def _dispatch(hn, dest, zero_blk, n_blocks, td):
    t = hn.shape[0]
    n_tiles = t // td
    grid_spec = pltpu.PrefetchScalarGridSpec(
        num_scalar_prefetch=1,
        grid=(n_tiles,),
        in_specs=[pl.BlockSpec((1, 1, 2 * td), lambda i, zb: (i, 0, 0), memory_space=pltpu.SMEM),
                  pl.BlockSpec((td, D_MODEL // 2), lambda i, zb: (i, 0))],
        out_specs=pl.BlockSpec(memory_space=pl.ANY),
        scratch_shapes=[pltpu.VMEM((MOE_BLOCK, D_MODEL // 2), jnp.uint32), pltpu.SemaphoreType.DMA((2,))],
    )
    return pl.pallas_call(
        functools.partial(_dispatch_kernel, td=td),
        grid_spec=grid_spec,
        out_shape=jax.ShapeDtypeStruct((n_blocks * MOE_BLOCK, D_MODEL // 2), jnp.uint32),
        compiler_params=_cparams("arbitrary"),
        name="dispatch",
    )(zero_blk, _dest_tiles(dest, n_tiles, td), hn)


def _expert_kernel(be_ref, nu_ref, x_ref, w1_ref, w3_ref, w2_ref, o_ref, w1b, w3b, w2b):
    i = pl.program_id(0)

    @pl.when((i == 0) | (be_ref[i] != be_ref[jnp.maximum(i - 1, 0)]))
    def _():
        w1b[...] = w1_ref[...].astype(BF16)
        w3b[...] = w3_ref[...].astype(BF16)
        w2b[...] = w2_ref[...].astype(BF16)

    @pl.when(i < nu_ref[0])
    def _():
        xb = _unpack_row_halves(x_ref[...]).astype(BF16)
        h1 = _dot(xb, w1b[...])
        h3 = _dot(xb, w3b[...])
        act = (h1 * _sigmoid(h1) * h3).astype(BF16)
        o_ref[...] = _pack_row_halves(_dot(act, w2b[...]))

    @pl.when(i >= nu_ref[0])
    def _():
        o_ref[...] = jnp.zeros_like(o_ref)


def _experts(layer, xg, block_e, n_used, n_blocks, w1, w3, w2):
    grid_spec = pltpu.PrefetchScalarGridSpec(
        num_scalar_prefetch=2,
        grid=(n_blocks,),
        in_specs=[
            pl.BlockSpec((MOE_BLOCK, D_MODEL // 2), lambda i, be, nu: (jnp.minimum(i, nu[0] - 1), 0)),
            pl.BlockSpec((None, None, D_MODEL, D_EXPERT), lambda i, be, nu: (layer, be[i], 0, 0)),
            pl.BlockSpec((None, None, D_MODEL, D_EXPERT), lambda i, be, nu: (layer, be[i], 0, 0)),
            pl.BlockSpec((None, None, D_EXPERT, D_MODEL), lambda i, be, nu: (layer, be[i], 0, 0)),
        ],
        out_specs=pl.BlockSpec((MOE_BLOCK, D_MODEL // 2), lambda i, be, nu: (i, 0)),
        scratch_shapes=[pltpu.VMEM((D_MODEL, D_EXPERT), BF16), pltpu.VMEM((D_MODEL, D_EXPERT), BF16),
                        pltpu.VMEM((D_EXPERT, D_MODEL), BF16)],
    )
    return pl.pallas_call(
        _expert_kernel,
        grid_spec=grid_spec,
        out_shape=jax.ShapeDtypeStruct((n_blocks * MOE_BLOCK, D_MODEL // 2), jnp.uint32),
        compiler_params=_cparams("arbitrary"),
        name="experts",
    )(block_e, n_used, xg, w1, w3, w2)


def _gather_rows(yg_hbm, dref, buf, sem, n_rows, unrolled):
    def start(r):
        pltpu.make_async_copy(yg_hbm.at[pl.ds(dref[0, 0, r], 1)], buf.at[pl.ds(r, 1)], sem).start()

    if unrolled:
        for r in range(n_rows):
            start(r)
        return

    def body(c, carry):
        for u in range(DISPATCH_UNROLL):
            start(c * DISPATCH_UNROLL + u)
        return carry
    lax.fori_loop(0, n_rows // DISPATCH_UNROLL, body, 0)


def _combine_compute(buf, x1_ref, rw_ref, p_ref, np_ref, wg_ref, wp_ref, tq):
    moe = (rw_ref[:, 0:1] * _unpack_row_halves(buf[0:tq, :])
           + rw_ref[:, 1:2] * _unpack_row_halves(buf[tq:2 * tq, :]))
    x2 = x1_ref[...] + moe
    g = _sigmoid(_dot(_rms(x2, np_ref[...]).astype(BF16), wg_ref[...]))
    return x2 + g * _dot(p_ref[...].astype(BF16), wp_ref[...])


def _combine_steps(i, n_tiles, tq, d_ref, dn_ref, yg_hbm, bufs, sem, unrolled, compute):
    def wait(s):
        pltpu.make_async_copy(yg_hbm.at[pl.ds(0, 2 * tq)], bufs[s], sem.at[s]).wait()

    @pl.when(i == 0)
    def _():
        _gather_rows(yg_hbm, d_ref, bufs[0], sem.at[0], 2 * tq, False)

    for parity in range(2):
        @pl.when(i % 2 == parity)
        def _():
            wait(parity)
            _gather_rows(yg_hbm, dn_ref, bufs[1 - parity], sem.at[1 - parity], 2 * tq, unrolled)
            compute(bufs[parity])

            @pl.when(i == n_tiles - 1)
            def _():
                wait(1 - parity)


def _combine_kernel(d_ref, dn_ref, yg_hbm, x1_ref, rw_ref, p_ref, np_ref, wg_ref, wp_ref, nfin_ref, o_ref,
                    buf0, buf1, sem, *, tq, n_tiles, final):
    def compute(buf):
        x3 = _combine_compute(buf, x1_ref, rw_ref, p_ref, np_ref, wg_ref, wp_ref, tq)
        o_ref[...] = _rms(x3, nfin_ref[...]) if final else x3

    _combine_steps(pl.program_id(0), n_tiles, tq, d_ref, dn_ref, yg_hbm, (buf0, buf1), sem, False, compute)


def _combine_mix_in_kernel(d_ref, dn_ref, yg_hbm, x1_ref, rw_ref, p_ref, np_ref, wg_ref, wp_ref,
                           nm_ref, w_ref, lng_ref, lnb_ref, wsp_ref, bsp_ref, dft_ref,
                           o_ref, ya_ref, xri_ref, yglu_ref, gates_ref, buf0, buf1, sem, *, tq, n_tiles):
    def compute(buf):
        x3 = _combine_compute(buf, x1_ref, rw_ref, p_ref, np_ref, wg_ref, wp_ref, tq)
        o_ref[...] = x3
        _mix_in_compute(x3, nm_ref, w_ref, lng_ref, lnb_ref, wsp_ref, bsp_ref, dft_ref,
                        ya_ref, xri_ref, yglu_ref, gates_ref, tq)

    _combine_steps(pl.program_id(0), n_tiles, tq, d_ref, dn_ref, yg_hbm, (buf0, buf1), sem, True, compute)


def _combine_in_specs(layer, tq, n_tiles):
    return [
        pl.BlockSpec((1, 1, 2 * tq), lambda i: (i, 0, 0), memory_space=pltpu.SMEM),
        pl.BlockSpec((1, 1, 2 * tq), lambda i: (jnp.minimum(i + 1, n_tiles - 1), 0, 0), memory_space=pltpu.SMEM),
        pl.BlockSpec(memory_space=pl.ANY),
        pl.BlockSpec((tq, D_MODEL), lambda i: (i, 0)),
        pl.BlockSpec((tq, LANES), lambda i: (i, 0)),
        pl.BlockSpec((None, tq, PLE_DIM), lambda i: (layer, i, 0)),
        _layer_spec(layer, (1, D_MODEL)), _layer_spec(layer, (D_MODEL, D_MODEL)), _layer_spec(layer, (PLE_DIM, D_MODEL)),
    ]


def _combine_scratch(tq):
    buf = pltpu.VMEM((2 * tq, D_MODEL // 2), jnp.uint32)
    return [buf, buf, pltpu.SemaphoreType.DMA((2,))]


def _combine(layer, dest, yg, x1, rw, p, norm_ple, wg, wp, nfin, tq, final):
    t = x1.shape[0]
    n_tiles = t // tq
    d3 = _dest_tiles(dest, n_tiles, tq)
    return pl.pallas_call(
        functools.partial(_combine_kernel, tq=tq, n_tiles=n_tiles, final=final),
        grid=(n_tiles,),
        in_specs=_combine_in_specs(layer, tq, n_tiles) + [_resident((1, D_MODEL))],
        out_specs=pl.BlockSpec((tq, D_MODEL), lambda i: (i, 0)),
        out_shape=jax.ShapeDtypeStruct((t, D_MODEL), F32),
        scratch_shapes=_combine_scratch(tq),
        compiler_params=_cparams("arbitrary"),
        name="combine",
    )(d3, d3, yg, x1, rw, p, norm_ple, wg, wp, nfin)


def _combine_mix_in(layer, dest, yg, x1, rw, p, norm_ple, wg, wp, mix_weights, tq):
    t = x1.shape[0]
    n_tiles = t // tq
    d3 = _dest_tiles(dest, n_tiles, tq)
    mix_specs, mix_shapes = _mix_in_out(t, tq)
    return pl.pallas_call(
        functools.partial(_combine_mix_in_kernel, tq=tq, n_tiles=n_tiles),
        grid=(n_tiles,),
        in_specs=_combine_in_specs(layer, tq, n_tiles) + _mix_in_weight_specs(layer + 1),
        out_specs=[pl.BlockSpec((tq, D_MODEL), lambda i: (i, 0))] + mix_specs,
        out_shape=[jax.ShapeDtypeStruct((t, D_MODEL), F32)] + mix_shapes,
        scratch_shapes=_combine_scratch(tq),
        compiler_params=_cparams("arbitrary"),
        name="combine_mix_in",
    )(d3, d3, yg, x1, rw, p, norm_ple, wg, wp, *mix_weights)


def _tile(t, want):
    return want if t % want == 0 else CHUNK


def _run_group(x, p, wts):
    bsz, seq, _ = x.shape
    depth = p.shape[0]
    t = bsz * seq
    n1 = seq // FFT_N2
    tm = _tile(t, 512)
    tq = _tile(t, 256)
    dftc, a_mat, twr, twi, w2f = _dft_constants(n1)
    mix_weights = (wts['norm_mix'], wts['w_in'], wts['gmlp_ln_g'], wts['gmlp_ln_b'], wts['wsp'], wts['bsp'], dftc)
    ple = (wts['norm_ple'], wts['w_ple_gate'], wts['w_ple_proj'])
    xf = x.reshape(t, D_MODEL)
    pf = p.reshape(depth, t, PLE_DIM)
    ya, xri, yglu, gates = _mix_in(0, xf, *mix_weights, tm)
    for l in range(depth):
        ypk = _fft1(xri, a_mat, twr, twi, bsz, n1)
        yb = _fft2(ypk, w2f, bsz, n1).reshape(t, D_B)
        x1, hn, ri, rw = _mix_out(l, xf, ya, yb, yglu, gates, wts['conv_w'], wts['conv_b'], wts['conv_ln_g'],
                                  wts['conv_ln_b'], wts['w_out_a'], wts['w_out_b'], wts['w_out_c'], wts['w_o'],
                                  wts['norm_ffn'], wts['wr_hi'], wts['wr_lo'], wts['br'], seq, tm)
        dest, block_e, n_used, zero_blk, n_blocks = _route_plan(ri, t)
        xg = _dispatch(hn, dest, zero_blk, n_blocks, tm)
        yg = _experts(l, xg, block_e, n_used, n_blocks, wts['w1'], wts['w3'], wts['w2'])
        if l + 1 < depth:
            xf, ya, xri, yglu, gates = _combine_mix_in(l, dest, yg, x1, rw, pf, *ple, mix_weights, tq)
        else:
            xf = _combine(l, dest, yg, x1, rw, pf, *ple, wts['norm_final'], tq, True)
    return xf.reshape(bsz, seq, D_MODEL)


def _prepare_weights(norm_mix, w_in, gmlp_ln_g, gmlp_ln_b, w_spatial, b_spatial, conv_w, conv_b, conv_ln_g,
                     conv_ln_b, w_out_a, w_out_b, w_out_c, w_o, norm_ffn, w_router_group, b_router_group,
                     w_router_expert, b_router_expert, w1, w3, w2, norm_ple, w_ple_gate, w_ple_proj, norm_final):
    depth = w_in.shape[0]
    row = lambda a: a.reshape(depth, 1, -1)
    wsp = jnp.concatenate([w_spatial[:, 0::2], w_spatial[:, 1::2]], axis=-1).astype(BF16)
    bpair = jnp.stack([b_spatial[:, 0::2], b_spatial[:, 1::2]], axis=-1)
    bsp = jnp.repeat(bpair, A_HEAD, axis=-1)
    conv_wp = jnp.broadcast_to(conv_w[:, :, None, :], (depth, CONV_WIDTH, SUBLANES, D_C))
    pad_g = jnp.zeros((depth, EXPERT_ROW0 - N_GROUPS, D_MODEL), F32)
    pad_e = jnp.zeros((depth, ROUTE_ROWS - EXPERT_ROW0 - N_EXPERTS, D_MODEL), F32)
    wr = jnp.concatenate([jnp.swapaxes(w_router_group, 1, 2), pad_g, jnp.swapaxes(w_router_expert, 1, 2), pad_e],
                         axis=1)
    br = jnp.concatenate([b_router_group, pad_g[:, :, 0], b_router_expert, pad_e[:, :, 0]], axis=1)[:, :, None]
    wr_hi = wr.astype(BF16)
    wr_lo = (wr - wr_hi.astype(F32)).astype(BF16)
    return dict(
        norm_mix=row(norm_mix), w_in=w_in.astype(BF16), gmlp_ln_g=row(gmlp_ln_g), gmlp_ln_b=row(gmlp_ln_b),
        wsp=wsp, bsp=bsp, conv_w=conv_wp, conv_b=row(conv_b), conv_ln_g=row(conv_ln_g), conv_ln_b=row(conv_ln_b),
        w_out_a=w_out_a.astype(BF16), w_out_b=w_out_b.astype(BF16), w_out_c=w_out_c.astype(BF16),
        w_o=w_o.astype(BF16), norm_ffn=row(norm_ffn), wr_hi=wr_hi, wr_lo=wr_lo, br=br,
        w1=w1, w3=w3, w2=w2,
        norm_ple=row(norm_ple), w_ple_gate=w_ple_gate.astype(BF16), w_ple_proj=w_ple_proj.astype(BF16),
        norm_final=norm_final.reshape(1, D_MODEL))


def kernel(x_prompt, x_sample, p_prompt, p_sample, norm_mix, w_in, gmlp_ln_g, gmlp_ln_b, w_spatial, b_spatial, conv_w, conv_b, conv_ln_g, conv_ln_b, w_out_a, w_out_b, w_out_c, w_o, norm_ffn, w_router_group, b_router_group, w_router_expert, b_router_expert, w1, w3, w2, norm_ple, w_ple_gate, w_ple_proj, norm_final):
    wts = _prepare_weights(norm_mix, w_in, gmlp_ln_g, gmlp_ln_b, w_spatial, b_spatial, conv_w, conv_b, conv_ln_g,
                           conv_ln_b, w_out_a, w_out_b, w_out_c, w_o, norm_ffn, w_router_group, b_router_group,
                           w_router_expert, b_router_expert, w1, w3, w2, norm_ple, w_ple_gate, w_ple_proj,
                           norm_final)
    return (_run_group(x_prompt, p_prompt, wts), _run_group(x_sample, p_sample, wts))
```

```python
import functools
import math

import numpy as np
import jax
import jax.numpy as jnp
from jax import lax
from jax.experimental import pallas as pl
from jax.experimental.pallas import tpu as pltpu

D_MODEL = 1024
CHUNK = 128
A_GROUPS = 8
A_HEAD = 64
D_A = A_GROUPS * A_HEAD
B_GROUPS = 4
B_HEAD = 128
D_B = B_GROUPS * B_HEAD
D_C = 512
CONV_WIDTH = 31
CONV_HALO = 16
N_BRANCH = 3
O1 = 2 * D_A
O2 = O1 + D_B
O3 = O2 + 2 * D_C
IN_WIDTH = O3 + N_BRANCH * D_MODEL
N_GROUPS = 4
E_PER_GROUP = 8
N_EXPERTS = N_GROUPS * E_PER_GROUP
D_EXPERT = 512
MOE_BLOCK = 256
PLE_DIM = 256
EPS = 1e-6

LANES = 128
SUBLANES = 8
FFT_N2 = 128
ROUTE_ROWS = 128
EXPERT_ROW0 = 8
RANK_TILE = 1024
VMEM_LIMIT = 56 * 1024 * 1024

F32 = jnp.float32
BF16 = jnp.bfloat16


def _cparams(*sem):
    return pltpu.CompilerParams(dimension_semantics=sem, vmem_limit_bytes=VMEM_LIMIT)


def _resident(shape, index=None):
    idx = tuple(index) if index is not None else (0,) * len(shape)
    return pl.BlockSpec(shape, lambda *_: idx, pipeline_mode=pl.Buffered(1))


def _rms(x, g):
    return x * lax.rsqrt(jnp.mean(x * x, axis=-1, keepdims=True) + EPS) * g


def _ln(x, g, b):
    mu = jnp.mean(x, axis=-1, keepdims=True)
    xc = x - mu
    return xc * lax.rsqrt(jnp.mean(xc * xc, axis=-1, keepdims=True) + EPS) * g + b


def _sigmoid(x):
    return 1.0 / (1.0 + jnp.exp(-x))


def _dot(a, b):
    return jnp.dot(a, b, preferred_element_type=F32)


def _pack_pair(a, b):
    hi = lax.bitcast_convert_type(a.astype(BF16).astype(F32), jnp.uint32)
    lo = lax.bitcast_convert_type(b.astype(BF16).astype(F32), jnp.uint32)
    return hi | (lo >> 16)


def _unpack_pair(p):
    a = lax.bitcast_convert_type(p & jnp.uint32(0xFFFF0000), F32)
    b = lax.bitcast_convert_type(p << 16, F32)
    return a, b


def _pack_row_halves(x):
    n = x.shape[1] // 2
    return _pack_pair(x[:, :n], x[:, n:])


def _unpack_row_halves(p):
    return jnp.concatenate(_unpack_pair(p), axis=1)


def _mix_in_compute(x, nm_ref, w_ref, lng_ref, lnb_ref, wsp_ref, bsp_ref, dft_ref,
                    ya_ref, xri_ref, yglu_ref, gates_ref, tm):
    hb = _rms(x, nm_ref[...]).astype(BF16)

    def proj(lo, hi):
        return _dot(hb, w_ref[:, lo:hi])

    za = proj(0, O1)
    za = 0.5 * za * (1.0 + lax.erf(za * (1.0 / math.sqrt(2.0))))
    u = za[:, :D_A]
    v = _ln(za[:, D_A:], lng_ref[...], lnb_ref[...])
    n_chunks = tm // CHUNK
    low_half = lax.broadcasted_iota(jnp.int32, (CHUNK, LANES), 1) < A_HEAD
    for j in range(A_GROUPS // 2):
        cols = slice(j * LANES, (j + 1) * LANES)
        pieces = []
        for c in range(n_chunks):
            vp = v[c * CHUNK:(c + 1) * CHUNK, cols]
            pieces.append(jnp.concatenate([jnp.where(low_half, vp, 0.0), jnp.where(low_half, 0.0, vp)], axis=0))
        rhs = jnp.concatenate(pieces, axis=1).astype(BF16)
        sv = _dot(wsp_ref[j], rhs)
        for c in range(n_chunks):
            rows = slice(c * CHUNK, (c + 1) * CHUNK)
            svc = sv[:, c * LANES:(c + 1) * LANES] + bsp_ref[j]
            ya_ref[rows, cols] = (u[rows, cols] * svc).astype(ya_ref.dtype)

    zb = proj(O1, O2).astype(BF16)
    for g in range(B_GROUPS):
        cols = slice(g * B_HEAD, (g + 1) * B_HEAD)
        c = _dot(zb[:, cols], dft_ref[...])
        xri_ref[:, cols] = _pack_pair(c[:, :B_HEAD], c[:, B_HEAD:])

    zc = proj(O2, O3)
    yglu_ref[...] = zc[:, :D_C] * _sigmoid(zc[:, D_C:])

    for b in range(N_BRANCH):
        cols = slice(b * D_MODEL, (b + 1) * D_MODEL)
        gates_ref[:, cols] = _sigmoid(proj(O3 + b * D_MODEL, O3 + (b + 1) * D_MODEL)).astype(gates_ref.dtype)


def _mix_in_kernel(x_ref, nm_ref, w_ref, lng_ref, lnb_ref, wsp_ref, bsp_ref, dft_ref,
                   ya_ref, xri_ref, yglu_ref, gates_ref, *, tm):
    _mix_in_compute(x_ref[...], nm_ref, w_ref, lng_ref, lnb_ref, wsp_ref, bsp_ref, dft_ref,
                    ya_ref, xri_ref, yglu_ref, gates_ref, tm)


def _layer_spec(layer, shape):
    return pl.BlockSpec((None,) + shape, lambda i: (layer,) + (0,) * len(shape), pipeline_mode=pl.Buffered(1))


def _mix_in_weight_specs(layer):
    return [_layer_spec(layer, (1, D_MODEL)), _layer_spec(layer, (D_MODEL, IN_WIDTH)), _layer_spec(layer, (1, D_A)),
            _layer_spec(layer, (1, D_A)), _layer_spec(layer, (A_GROUPS // 2, CHUNK, 2 * CHUNK)),
            _layer_spec(layer, (A_GROUPS // 2, CHUNK, LANES)), _resident((B_HEAD, 2 * B_HEAD))]


def _mix_in_out(t, tm):
    def row(w):
        return pl.BlockSpec((tm, w), lambda i: (i, 0))
    specs = [row(D_A), row(D_B), row(D_C), row(N_BRANCH * D_MODEL)]
    shapes = [jax.ShapeDtypeStruct((t, D_A), BF16), jax.ShapeDtypeStruct((t, D_B), jnp.uint32),
              jax.ShapeDtypeStruct((t, D_C), F32), jax.ShapeDtypeStruct((t, N_BRANCH * D_MODEL), BF16)]
    return specs, shapes


def _mix_in(layer, x, nm, w_in, lng, lnb, wsp, bsp, dftc, tm):
    t = x.shape[0]
    out_specs, out_shape = _mix_in_out(t, tm)
    return pl.pallas_call(
        functools.partial(_mix_in_kernel, tm=tm),
        grid=(t // tm,),
        in_specs=[pl.BlockSpec((tm, D_MODEL), lambda i: (i, 0))] + _mix_in_weight_specs(layer),
        out_specs=out_specs,
        out_shape=out_shape,
        compiler_params=_cparams("parallel"),
        name="mix_in",
    )(x, nm, w_in, lng, lnb, wsp, bsp, dftc)


FFT_SLABS = SUBLANES


def _packed_rows(p):
    re, im = _unpack_pair(p)
    return jnp.concatenate([re, im], axis=0).astype(BF16)


def _fft1_kernel(x_hbm, a_ref, twr_ref, twi_ref, y_hbm, inbuf, outbuf, sem_in, sem_out, *, n1, nj, n_steps):
    b = pl.program_id(0)
    jb = pl.program_id(1)
    step = b * nj + jb
    slot = step % 2

    def in_copy(bb, jj, j, s):
        return pltpu.make_async_copy(x_hbm.at[bb, :, jj * FFT_SLABS + j, :], inbuf.at[s, j], sem_in.at[s])

    def out_copy(bb, jj, j, s):
        return pltpu.make_async_copy(outbuf.at[s, j], y_hbm.at[bb, :, jj * FFT_SLABS + j, :], sem_out.at[s])

    @pl.when(step == 0)
    def _():
        for j in range(FFT_SLABS):
            in_copy(b, jb, j, 0).start()

    @pl.when(step + 1 < n_steps)
    def _():
        nxt = step + 1
        for j in range(FFT_SLABS):
            in_copy(nxt // nj, nxt % nj, j, 1 - slot).start()

    @pl.when(step >= 2)
    def _():
        for j in range(FFT_SLABS):
            out_copy(b, jb, j, slot).wait()

    for j in range(FFT_SLABS):
        in_copy(b, jb, j, slot).wait()
    for j in range(FFT_SLABS):
        y = _dot(a_ref[...], _packed_rows(inbuf[slot, j]))
        yr, yi = y[:n1], y[n1:]
        tr, ti = twr_ref[j], twi_ref[j]
        outbuf[slot, j] = _pack_pair(yr * tr - yi * ti, yr * ti + yi * tr)
        out_copy(b, jb, j, slot).start()

    @pl.when(step == n_steps - 1)
    def _():
        for j in range(FFT_SLABS):
            out_copy(b, jb, j, slot).wait()
        for j in range(FFT_SLABS):
            out_copy(b, jb, j, 1 - slot).wait()


def _fft1(xri, a_mat, twr, twi, bsz, n1):
    n2 = FFT_N2
    nj = n2 // FFT_SLABS
    x4 = xri.reshape(bsz, n1, n2, D_B)
    tw = pl.BlockSpec((FFT_SLABS, n1, 1), lambda b, j: (j, 0, 0))
    return pl.pallas_call(
        functools.partial(_fft1_kernel, n1=n1, nj=nj, n_steps=bsz * nj),
        grid=(bsz, nj),
        in_specs=[pl.BlockSpec(memory_space=pl.ANY), _resident((2 * n1, 2 * n1)), tw, tw],
        out_specs=pl.BlockSpec(memory_space=pl.ANY),
        out_shape=jax.ShapeDtypeStruct((bsz, n1, n2, D_B), jnp.uint32),
        scratch_shapes=[pltpu.VMEM((2, FFT_SLABS, n1, D_B), jnp.uint32), pltpu.VMEM((2, FFT_SLABS, n1, D_B), jnp.uint32),
                        pltpu.SemaphoreType.DMA((2,)), pltpu.SemaphoreType.DMA((2,))],
        compiler_params=_cparams("arbitrary", "arbitrary"),
        name="fft1",
    )(x4, a_mat, twr, twi)


def _fft2_kernel(y_ref, w_ref, o_hbm, outbuf, sem, *, nk, n_steps):
    b = pl.program_id(0)
    kb = pl.program_id(1)
    step = b * nk + kb
    slot = step % 2

    def out_copy(kk, s):
        return pltpu.make_async_copy(outbuf.at[s, kk], o_hbm.at[b, :, kb * FFT_SLABS + kk, :], sem.at[s])

    @pl.when(step >= 2)
    def _():
        for kk in range(FFT_SLABS):
            out_copy(kk, slot).wait()

    for kk in range(FFT_SLABS):
        outbuf[slot, kk] = _dot(w_ref[...], _packed_rows(y_ref[kk]))
        out_copy(kk, slot).start()

    @pl.when(step == n_steps - 1)
    def _():
        for kk in range(FFT_SLABS):
            out_copy(kk, slot).wait()
        for kk in range(FFT_SLABS):
            out_copy(kk, 1 - slot).wait()


def _fft2(ypk, w2, bsz, n1):
    n2 = FFT_N2
    nk = n1 // FFT_SLABS
    return pl.pallas_call(
        functools.partial(_fft2_kernel, nk=nk, n_steps=bsz * nk),
        grid=(bsz, nk),
        in_specs=[pl.BlockSpec((None, FFT_SLABS, n2, D_B), lambda b, k: (b, k, 0, 0)), _resident((n2, 2 * n2))],
        out_specs=pl.BlockSpec(memory_space=pl.ANY),
        out_shape=jax.ShapeDtypeStruct((bsz, n2, n1, D_B), F32),
        scratch_shapes=[pltpu.VMEM((2, FFT_SLABS, n2, D_B), F32), pltpu.SemaphoreType.DMA((2,))],
        compiler_params=_cparams("arbitrary", "arbitrary"),
        name="fft2",
    )(ypk, w2)


def _dft_constants(n1):
    n2 = FFT_N2
    k = np.arange(B_HEAD)
    ang = 2.0 * np.pi * np.outer(k, k) / B_HEAD
    dftc = np.concatenate([np.cos(ang), -np.sin(ang)], axis=1) / math.sqrt(B_HEAD)
    k1 = np.arange(n1)
    ang1 = 2.0 * np.pi * np.outer(k1, k1) / n1
    ar, ai = np.cos(ang1) / math.sqrt(n1), -np.sin(ang1) / math.sqrt(n1)
    a_mat = np.block([[ar, -ai], [ai, ar]])
    n2i = np.arange(n2)
    angt = 2.0 * np.pi * np.outer(n2i, k1) / (n1 * n2)
    twr, twi = np.cos(angt)[:, :, None], -np.sin(angt)[:, :, None]
    ang2 = 2.0 * np.pi * np.outer(n2i, n2i) / n2
    w2 = np.concatenate([np.cos(ang2), np.sin(ang2)], axis=1) / math.sqrt(n2)
    return (jnp.asarray(dftc, BF16), jnp.asarray(a_mat, BF16), jnp.asarray(twr, F32), jnp.asarray(twi, F32),
            jnp.asarray(w2, BF16))


CONV_ROWS = 32


def _conv_compute(i, cur_ref, prev_ref, next_ref, w_ref, b_ref, g_ref, bb_ref, o_ref, sh_ref, tc, seq):
    first = (i * tc) % seq == 0
    last = ((i + 1) * tc) % seq == 0
    n_ext = tc + 2 * CONV_HALO
    ext = jnp.concatenate([jnp.where(first, 0.0, prev_ref[...]), cur_ref[...],
                           jnp.where(last, 0.0, next_ref[...])], axis=0)
    sh_ref[0] = ext
    for s in range(1, SUBLANES):
        sh_ref[s] = pltpu.roll(ext, n_ext - s, axis=0)
    base = CONV_HALO - CONV_WIDTH // 2
    groups = CONV_ROWS // SUBLANES
    for r in range(tc // CONV_ROWS):
        acc = [jnp.zeros((SUBLANES, D_C), F32) for _ in range(groups)]
        for k in range(CONV_WIDTH):
            s = (base + k) % SUBLANES
            lo = r * CONV_ROWS + base + k - s
            wk = w_ref[k]
            for q in range(groups):
                acc[q] = acc[q] + wk * sh_ref[s, lo + q * SUBLANES:lo + (q + 1) * SUBLANES, :]
        y = _ln(jnp.concatenate(acc, axis=0) + b_ref[...], g_ref[...], bb_ref[...])
        o_ref[r * CONV_ROWS:(r + 1) * CONV_ROWS] = (y * _sigmoid(y)).astype(o_ref.dtype)


def _mix_out_kernel(x_ref, ya_ref, yb_ref, glu_ref, glu_prev_ref, glu_next_ref, gt_ref,
                    cw_ref, cb_ref, cg_ref, cbb_ref, wa_ref, wb_ref, wc_ref, wo_ref, nf_ref,
                    wrh_ref, wrl_ref, br_ref, x1_ref, hn_ref, ri_ref, rw_ref, yc_ref, sh_ref, *, tm, seq):
    _conv_compute(pl.program_id(0), glu_ref, glu_prev_ref, glu_next_ref, cw_ref, cb_ref, cg_ref, cbb_ref,
                  yc_ref, sh_ref, tm, seq)
    m = gt_ref[:, 0:D_MODEL].astype(F32) * _dot(ya_ref[...], wa_ref[...])
    m = m + gt_ref[:, D_MODEL:2 * D_MODEL].astype(F32) * _dot(yb_ref[...].astype(BF16), wb_ref[...])
    m = m + gt_ref[:, 2 * D_MODEL:3 * D_MODEL].astype(F32) * _dot(yc_ref[...], wc_ref[...])
    x1 = x_ref[...] + _dot(m.astype(BF16), wo_ref[...])
    x1_ref[...] = x1
    hn = _rms(x1, nf_ref[...])
    hn_ref[...] = _pack_row_halves(hn)

    hn_hi = hn.astype(BF16)
    hn_lo = (hn - hn_hi.astype(F32)).astype(BF16)
    nt = (((1,), (1,)), ((), ()))
    lg = (lax.dot_general(wrh_ref[...], hn_hi, nt, preferred_element_type=F32)
          + lax.dot_general(wrh_ref[...], hn_lo, nt, preferred_element_type=F32)
          + lax.dot_general(wrl_ref[...], hn_hi, nt, preferred_element_type=F32)) + br_ref[...]
    gl = [lg[g:g + 1, :] for g in range(N_GROUPS)]
    best = gl[0]
    gsel = jnp.zeros((1, tm), jnp.int32)
    for g in range(1, N_GROUPS):
        better = gl[g] > best
        best = jnp.where(better, gl[g], best)
        gsel = jnp.where(better, g, gsel)
    denom = jnp.exp(gl[0] - best)
    for g in range(1, N_GROUPS):
        denom = denom + jnp.exp(gl[g] - best)
    pg = 1.0 / denom
    es = lg[EXPERT_ROW0:EXPERT_ROW0 + E_PER_GROUP, :]
    for g in range(1, N_GROUPS):
        lo = EXPERT_ROW0 + g * E_PER_GROUP
        es = jnp.where(gsel == g, lg[lo:lo + E_PER_GROUP, :], es)
    slot = lax.broadcasted_iota(jnp.int32, (E_PER_GROUP, tm), 0)
    v0 = jnp.max(es, axis=0, keepdims=True)
    i0 = jnp.min(jnp.where(es == v0, slot, E_PER_GROUP), axis=0, keepdims=True)
    es1 = jnp.where(slot == i0, -jnp.inf, es)
    v1 = jnp.max(es1, axis=0, keepdims=True)
    i1 = jnp.min(jnp.where(es1 == v1, slot, E_PER_GROUP), axis=0, keepdims=True)
    ex = jnp.exp(v1 - v0)
    p0 = 1.0 / (1.0 + ex)
    p1 = ex / (1.0 + ex)
    row = lax.broadcasted_iota(jnp.int32, (SUBLANES, tm), 0)
    e0 = gsel * E_PER_GROUP + i0
    e1 = gsel * E_PER_GROUP + i1
    ri_ref[...] = jnp.where(row == 0, e0, jnp.where(row == 1, e1, 0))
    wrow = lax.broadcasted_iota(jnp.int32, (LANES, tm), 0)
    rw_ref[...] = jnp.where(wrow == 0, pg * p0, jnp.where(wrow == 1, pg * p1, 0.0)).T


def _mix_out(layer, x, ya, yb, yglu, gates, conv_w, conv_b, conv_lng, conv_lnb, wa, wb, wc, wo, nf,
             wr_hi, wr_lo, br, seq, tm):
    t = x.shape[0]
    hb = tm // CONV_HALO
    n_halo = t // CONV_HALO

    def row(w):
        return pl.BlockSpec((tm, w), lambda i: (i, 0))

    def lay(shape):
        return _layer_spec(layer, shape)

    return pl.pallas_call(
        functools.partial(_mix_out_kernel, tm=tm, seq=seq),
        grid=(t // tm,),
        in_specs=[row(D_MODEL), row(D_A), row(D_B), row(D_C),
                  pl.BlockSpec((CONV_HALO, D_C), lambda i: (jnp.maximum(i * hb - 1, 0), 0)),
                  pl.BlockSpec((CONV_HALO, D_C), lambda i: (jnp.minimum((i + 1) * hb, n_halo - 1), 0)),
                  row(N_BRANCH * D_MODEL),
                  lay((CONV_WIDTH, SUBLANES, D_C)), lay((1, D_C)), lay((1, D_C)), lay((1, D_C)),
                  lay((D_A, D_MODEL)), lay((D_B, D_MODEL)), lay((D_C, D_MODEL)), lay((D_MODEL, D_MODEL)),
                  lay((1, D_MODEL)), lay((ROUTE_ROWS, D_MODEL)), lay((ROUTE_ROWS, D_MODEL)), lay((ROUTE_ROWS, 1))],
        out_specs=[row(D_MODEL), row(D_MODEL // 2), pl.BlockSpec((SUBLANES, tm), lambda i: (0, i)), row(LANES)],
        out_shape=[jax.ShapeDtypeStruct((t, D_MODEL), F32), jax.ShapeDtypeStruct((t, D_MODEL // 2), jnp.uint32),
                   jax.ShapeDtypeStruct((SUBLANES, t), jnp.int32), jax.ShapeDtypeStruct((t, LANES), F32)],
        scratch_shapes=[pltpu.VMEM((tm, D_C), BF16), pltpu.VMEM((SUBLANES, tm + 2 * CONV_HALO, D_C), F32)],
        compiler_params=_cparams("parallel"),
        name="mix_out",
    )(x, ya, yb, yglu, yglu, yglu, gates, conv_w, conv_b, conv_lng, conv_lnb, wa, wb, wc, wo, nf, wr_hi, wr_lo, br)


def _rank_kernel(e_ref, u_ref, rank_ref, cnt_ref, run_ref):
    k = pl.program_id(0)
    j = pl.program_id(1)

    @pl.when((k == 0) & (j == 0))
    def _():
        run_ref[...] = jnp.zeros_like(run_ref)

    tl = e_ref.shape[1]
    e = jnp.where(k == 0, e_ref[0:1, :], e_ref[1:2, :])
    onehot = lax.broadcasted_iota(jnp.int32, (N_EXPERTS, tl), 0) == e
    csum = _dot(jnp.where(onehot, 1.0, 0.0).astype(BF16), u_ref[...])
    base = run_ref[:, 0:1]
    rank = jnp.sum(jnp.where(onehot, csum - 1.0 + base, 0.0), axis=0, keepdims=True)
    rank_ref[...] = rank.astype(jnp.int32)
    run_ref[...] = run_ref[...] + csum[:, tl - 1:tl]
    cnt_ref[...] = run_ref[...]


def _route_plan(ri, t):
    tl = RANK_TILE if t % RANK_TILE == 0 else CHUNK
    upper = jnp.asarray(np.triu(np.ones((tl, tl), np.float32)), BF16)
    rank, cnt = pl.pallas_call(
        _rank_kernel,
        grid=(2, t // tl),
        in_specs=[pl.BlockSpec((SUBLANES, tl), lambda k, j: (0, j)), _resident((tl, tl))],
        out_specs=[pl.BlockSpec((None, 1, tl), lambda k, j: (k, 0, j)), _resident((N_EXPERTS, LANES))],
        out_shape=[jax.ShapeDtypeStruct((2, 1, t), jnp.int32), jax.ShapeDtypeStruct((N_EXPERTS, LANES), F32)],
        scratch_shapes=[pltpu.VMEM((N_EXPERTS, LANES), F32)],
        compiler_params=_cparams("arbitrary", "arbitrary"),
        name="rank",
    )(ri, upper)
    counts = cnt[:, 0].astype(jnp.int32)
    padded = (counts + MOE_BLOCK - 1) // MOE_BLOCK * MOE_BLOCK
    pad_end = jnp.cumsum(padded)
    pad_start = pad_end - padded
    experts = jnp.arange(N_EXPERTS, dtype=jnp.int32)
    dest = rank[:, 0, :] + jnp.sum(jnp.where(ri[:2, :, None] == experts, pad_start, 0), axis=-1)
    n_blocks = 2 * t // MOE_BLOCK + N_EXPERTS
    blk_start = (pad_start // MOE_BLOCK).astype(jnp.int32)
    blk_count = (padded // MOE_BLOCK).astype(jnp.int32)
    n_used = (pad_end[-1] // MOE_BLOCK).astype(jnp.int32).reshape(1)
    tail = pad_end[-1] + experts * MOE_BLOCK
    zero_blk = jnp.concatenate([jnp.where(padded > 0, pad_end - MOE_BLOCK, -1),
                                jnp.where(tail < n_blocks * MOE_BLOCK, tail, -1)]).astype(jnp.int32)
    return dest.astype(jnp.int32), blk_start, blk_count, n_used, zero_blk, n_blocks


DISPATCH_UNROLL = 8


def _dispatch_kernel(zb_ref, d_ref, hn_ref, xg_hbm, zero_ref, sem, *, td):
    i = pl.program_id(0)

    def zero_copy(e):
        start = pl.multiple_of(zb_ref[e], MOE_BLOCK)
        return pltpu.make_async_copy(zero_ref, xg_hbm.at[pl.ds(start, MOE_BLOCK)], sem.at[1])

    @pl.when(i == 0)
    def _():
        zero_ref[...] = jnp.zeros_like(zero_ref)
        for e in range(2 * N_EXPERTS):
            @pl.when(zb_ref[e] >= 0)
            def _():
                zero_copy(e).start()
        for e in range(2 * N_EXPERTS):
            @pl.when(zb_ref[e] >= 0)
            def _():
                zero_copy(e).wait()

    def body(c, carry):
        for u in range(DISPATCH_UNROLL):
            r = c * DISPATCH_UNROLL + u
            src = hn_ref.at[pl.ds(r, 1)]
            pltpu.make_async_copy(src, xg_hbm.at[pl.ds(d_ref[0, 0, r], 1)], sem.at[0]).start()
            pltpu.make_async_copy(src, xg_hbm.at[pl.ds(d_ref[0, 0, td + r], 1)], sem.at[0]).start()
        return carry

    lax.fori_loop(0, td // DISPATCH_UNROLL, body, 0)
    for _ in range(2):
        pltpu.make_async_copy(hn_ref, xg_hbm.at[pl.ds(0, td)], sem.at[0]).wait()


def _dest_tiles(dest, n_tiles, tile):
    return dest.reshape(2, n_tiles, tile).transpose(1, 0, 2).reshape(n_tiles, 1, 2 * tile)


def _dispatch(hn, dest, zero_blk, n_blocks, td):
    t = hn.shape[0]
    n_tiles = t // td
    grid_spec = pltpu.PrefetchScalarGridSpec(
        num_scalar_prefetch=1,
        grid=(n_tiles,),
        in_specs=[pl.BlockSpec((1, 1, 2 * td), lambda i, zb: (i, 0, 0), memory_space=pltpu.SMEM),
                  pl.BlockSpec((td, D_MODEL // 2), lambda i, zb: (i, 0))],
        out_specs=pl.BlockSpec(memory_space=pl.ANY),
        scratch_shapes=[pltpu.VMEM((MOE_BLOCK, D_MODEL // 2), jnp.uint32), pltpu.SemaphoreType.DMA((2,))],
    )
    return pl.pallas_call(
        functools.partial(_dispatch_kernel, td=td),
        grid_spec=grid_spec,
        out_shape=jax.ShapeDtypeStruct((n_blocks * MOE_BLOCK, D_MODEL // 2), jnp.uint32),
        compiler_params=_cparams("arbitrary"),
        name="dispatch",
    )(zero_blk, _dest_tiles(dest, n_tiles, td), hn)


def _expert_kernel(bs_ref, bc_ref, nu_ref, xg_hbm, w1_ref, w3_ref, w2_ref, yg_hbm, xbuf, obuf, w1b, w3b, w2b,
                   sem_in, sem_out, *, n_blocks):
    e = pl.program_id(0)
    first = bs_ref[e]
    n = bc_ref[e]

    def rows(blk):
        return pl.ds(pl.multiple_of(blk * MOE_BLOCK, MOE_BLOCK), MOE_BLOCK)

    def in_copy(blk, s):
        return pltpu.make_async_copy(xg_hbm.at[rows(blk)], xbuf.at[s], sem_in.at[s])

    def out_copy(blk, s):
        return pltpu.make_async_copy(obuf.at[s], yg_hbm.at[rows(blk)], sem_out.at[s])

    @pl.when(n > 0)
    def _():
        in_copy(first, 0).start()
        w1b[...] = w1_ref[...].astype(BF16)
        w3b[...] = w3_ref[...].astype(BF16)
        w2b[...] = w2_ref[...].astype(BF16)

    def body(k, carry):
        s = k % 2

        @pl.when(k + 1 < n)
        def _():
            in_copy(first + k + 1, 1 - s).start()

        in_copy(first + k, s).wait()

        @pl.when(k >= 2)
        def _():
            out_copy(first + k, s).wait()

        xb = _unpack_row_halves(xbuf[s]).astype(BF16)
        h1 = _dot(xb, w1b[...])
        h3 = _dot(xb, w3b[...])
        act = (h1 * _sigmoid(h1) * h3).astype(BF16)
        obuf[s] = _pack_row_halves(_dot(act, w2b[...]))
        out_copy(first + k, s).start()
        return carry

    lax.fori_loop(0, n, body, 0)

    @pl.when(n >= 1)
    def _():
        out_copy(first, (n - 1) % 2).wait()

    @pl.when(n >= 2)
    def _():
        out_copy(first, n % 2).wait()

    @pl.when(e == N_EXPERTS - 1)
    def _():
        obuf[0] = jnp.zeros((MOE_BLOCK, D_MODEL // 2), jnp.uint32)

        def zero_block(blk, carry):
            cp = out_copy(blk, 0)
            cp.start()
            cp.wait()
            return carry

        lax.fori_loop(nu_ref[0], n_blocks, zero_block, 0)


def _experts(layer, xg, blk_start, blk_count, n_used, n_blocks, w1, w3, w2):
    def weight(shape):
        return pl.BlockSpec((None, None) + shape, lambda e, bs, bc, nu: (layer, e, 0, 0))

    blk = pltpu.VMEM((2, MOE_BLOCK, D_MODEL // 2), jnp.uint32)
    grid_spec = pltpu.PrefetchScalarGridSpec(
        num_scalar_prefetch=3,
        grid=(N_EXPERTS,),
        in_specs=[pl.BlockSpec(memory_space=pl.ANY), weight((D_MODEL, D_EXPERT)), weight((D_MODEL, D_EXPERT)),
                  weight((D_EXPERT, D_MODEL))],
        out_specs=pl.BlockSpec(memory_space=pl.ANY),
        scratch_shapes=[blk, blk, pltpu.VMEM((D_MODEL, D_EXPERT), BF16), pltpu.VMEM((D_MODEL, D_EXPERT), BF16),
                        pltpu.VMEM((D_EXPERT, D_MODEL), BF16), pltpu.SemaphoreType.DMA((2,)),
                        pltpu.SemaphoreType.DMA((2,))],
    )
    return pl.pallas_call(
        functools.partial(_expert_kernel, n_blocks=n_blocks),
        grid_spec=grid_spec,
        out_shape=jax.ShapeDtypeStruct((n_blocks * MOE_BLOCK, D_MODEL // 2), jnp.uint32),
        compiler_params=_cparams("arbitrary"),
        name="experts",
    )(blk_start, blk_count, n_used, xg, w1, w3, w2)


def _gather_rows(yg_hbm, dref, buf, sem, n_rows, unrolled):
    def start(r):
        pltpu.make_async_copy(yg_hbm.at[pl.ds(dref[0, 0, r], 1)], buf.at[pl.ds(r, 1)], sem).start()

    if unrolled:
        for r in range(n_rows):
            start(r)
        return

    def body(c, carry):
        for u in range(DISPATCH_UNROLL):
            start(c * DISPATCH_UNROLL + u)
        return carry
    lax.fori_loop(0, n_rows // DISPATCH_UNROLL, body, 0)


def _combine_compute(buf, x1_ref, rw_ref, p_ref, np_ref, wg_ref, wp_ref, tq):
    moe = (rw_ref[:, 0:1] * _unpack_row_halves(buf[0:tq, :])
           + rw_ref[:, 1:2] * _unpack_row_halves(buf[tq:2 * tq, :]))
    x2 = x1_ref[...] + moe
    g = _sigmoid(_dot(_rms(x2, np_ref[...]).astype(BF16), wg_ref[...]))
    return x2 + g * _dot(p_ref[...].astype(BF16), wp_ref[...])


def _combine_steps(i, n_tiles, tq, d_ref, dn_ref, yg_hbm, bufs, sem, unrolled, compute):
    def wait(s):
        pltpu.make_async_copy(yg_hbm.at[pl.ds(0, 2 * tq)], bufs[s], sem.at[s]).wait()

    @pl.when(i == 0)
    def _():
        _gather_rows(yg_hbm, d_ref, bufs[0], sem.at[0], 2 * tq, False)

    for parity in range(2):
        @pl.when(i % 2 == parity)
        def _():
            wait(parity)
            _gather_rows(yg_hbm, dn_ref, bufs[1 - parity], sem.at[1 - parity], 2 * tq, unrolled)
            compute(bufs[parity])

            @pl.when(i == n_tiles - 1)
            def _():
                wait(1 - parity)


def _combine_kernel(d_ref, dn_ref, yg_hbm, x1_ref, rw_ref, p_ref, np_ref, wg_ref, wp_ref, nfin_ref, o_ref,
                    buf0, buf1, sem, *, tq, n_tiles, final):
    def compute(buf):
        x3 = _combine_compute(buf, x1_ref, rw_ref, p_ref, np_ref, wg_ref, wp_ref, tq)
        o_ref[...] = _rms(x3, nfin_ref[...]) if final else x3

    _combine_steps(pl.program_id(0), n_tiles, tq, d_ref, dn_ref, yg_hbm, (buf0, buf1), sem, False, compute)


def _combine_mix_in_kernel(d_ref, dn_ref, yg_hbm, x1_ref, rw_ref, p_ref, np_ref, wg_ref, wp_ref,
                           nm_ref, w_ref, lng_ref, lnb_ref, wsp_ref, bsp_ref, dft_ref,
                           o_ref, ya_ref, xri_ref, yglu_ref, gates_ref, buf0, buf1, sem, *, tq, n_tiles):
    def compute(buf):
        x3 = _combine_compute(buf, x1_ref, rw_ref, p_ref, np_ref, wg_ref, wp_ref, tq)
        o_ref[...] = x3
        _mix_in_compute(x3, nm_ref, w_ref, lng_ref, lnb_ref, wsp_ref, bsp_ref, dft_ref,
                        ya_ref, xri_ref, yglu_ref, gates_ref, tq)

    _combine_steps(pl.program_id(0), n_tiles, tq, d_ref, dn_ref, yg_hbm, (buf0, buf1), sem, True, compute)


def _combine_in_specs(layer, tq, n_tiles):
    return [
        pl.BlockSpec((1, 1, 2 * tq), lambda i: (i, 0, 0), memory_space=pltpu.SMEM),
        pl.BlockSpec((1, 1, 2 * tq), lambda i: (jnp.minimum(i + 1, n_tiles - 1), 0, 0), memory_space=pltpu.SMEM),
        pl.BlockSpec(memory_space=pl.ANY),
        pl.BlockSpec((tq, D_MODEL), lambda i: (i, 0)),
        pl.BlockSpec((tq, LANES), lambda i: (i, 0)),
        pl.BlockSpec((None, tq, PLE_DIM), lambda i: (layer, i, 0)),
        _layer_spec(layer, (1, D_MODEL)), _layer_spec(layer, (D_MODEL, D_MODEL)), _layer_spec(layer, (PLE_DIM, D_MODEL)),
    ]


def _combine_scratch(tq):
    buf = pltpu.VMEM((2 * tq, D_MODEL // 2), jnp.uint32)
    return [buf, buf, pltpu.SemaphoreType.DMA((2,))]


def _combine(layer, dest, yg, x1, rw, p, norm_ple, wg, wp, nfin, tq, final):
    t = x1.shape[0]
    n_tiles = t // tq
    d3 = _dest_tiles(dest, n_tiles, tq)
    return pl.pallas_call(
        functools.partial(_combine_kernel, tq=tq, n_tiles=n_tiles, final=final),
        grid=(n_tiles,),
        in_specs=_combine_in_specs(layer, tq, n_tiles) + [_resident((1, D_MODEL))],
        out_specs=pl.BlockSpec((tq, D_MODEL), lambda i: (i, 0)),
        out_shape=jax.ShapeDtypeStruct((t, D_MODEL), F32),
        scratch_shapes=_combine_scratch(tq),
        compiler_params=_cparams("arbitrary"),
        name="combine",
    )(d3, d3, yg, x1, rw, p, norm_ple, wg, wp, nfin)


def _combine_mix_in(layer, dest, yg, x1, rw, p, norm_ple, wg, wp, mix_weights, tq):
    t = x1.shape[0]
    n_tiles = t // tq
    d3 = _dest_tiles(dest, n_tiles, tq)
    mix_specs, mix_shapes = _mix_in_out(t, tq)
    return pl.pallas_call(
        functools.partial(_combine_mix_in_kernel, tq=tq, n_tiles=n_tiles),
        grid=(n_tiles,),
        in_specs=_combine_in_specs(layer, tq, n_tiles) + _mix_in_weight_specs(layer + 1),
        out_specs=[pl.BlockSpec((tq, D_MODEL), lambda i: (i, 0))] + mix_specs,
        out_shape=[jax.ShapeDtypeStruct((t, D_MODEL), F32)] + mix_shapes,
        scratch_shapes=_combine_scratch(tq),
        compiler_params=_cparams("arbitrary"),
        name="combine_mix_in",
    )(d3, d3, yg, x1, rw, p, norm_ple, wg, wp, *mix_weights)


def _tile(t, want):
    return want if t % want == 0 else CHUNK


def _run_group(x, p, wts):
    bsz, seq, _ = x.shape
    depth = p.shape[0]
    t = bsz * seq
    n1 = seq // FFT_N2
    tm = _tile(t, 512)
    tq = _tile(t, 256)
    dftc, a_mat, twr, twi, w2f = _dft_constants(n1)
    mix_weights = (wts['norm_mix'], wts['w_in'], wts['gmlp_ln_g'], wts['gmlp_ln_b'], wts['wsp'], wts['bsp'], dftc)
    ple = (wts['norm_ple'], wts['w_ple_gate'], wts['w_ple_proj'])
    xf = x.reshape(t, D_MODEL)
    pf = p.reshape(depth, t, PLE_DIM)
    ya, xri, yglu, gates = _mix_in(0, xf, *mix_weights, tm)
    for l in range(depth):
        ypk = _fft1(xri, a_mat, twr, twi, bsz, n1)
        yb = _fft2(ypk, w2f, bsz, n1).reshape(t, D_B)
        x1, hn, ri, rw = _mix_out(l, xf, ya, yb, yglu, gates, wts['conv_w'], wts['conv_b'], wts['conv_ln_g'],
                                  wts['conv_ln_b'], wts['w_out_a'], wts['w_out_b'], wts['w_out_c'], wts['w_o'],
                                  wts['norm_ffn'], wts['wr_hi'], wts['wr_lo'], wts['br'], seq, tm)
        dest, blk_start, blk_count, n_used, zero_blk, n_blocks = _route_plan(ri, t)
        xg = _dispatch(hn, dest, zero_blk, n_blocks, _tile(t, 1024))
        yg = _experts(l, xg, blk_start, blk_count, n_used, n_blocks, wts['w1'], wts['w3'], wts['w2'])
        if l + 1 < depth:
            xf, ya, xri, yglu, gates = _combine_mix_in(l, dest, yg, x1, rw, pf, *ple, mix_weights, tq)
        else:
            xf = _combine(l, dest, yg, x1, rw, pf, *ple, wts['norm_final'], tq, True)
    return xf.reshape(bsz, seq, D_MODEL)


def _prepare_weights(norm_mix, w_in, gmlp_ln_g, gmlp_ln_b, w_spatial, b_spatial, conv_w, conv_b, conv_ln_g,
                     conv_ln_b, w_out_a, w_out_b, w_out_c, w_o, norm_ffn, w_router_group, b_router_group,
                     w_router_expert, b_router_expert, w1, w3, w2, norm_ple, w_ple_gate, w_ple_proj, norm_final):
    depth = w_in.shape[0]
    row = lambda a: a.reshape(depth, 1, -1)
    wsp = jnp.concatenate([w_spatial[:, 0::2], w_spatial[:, 1::2]], axis=-1).astype(BF16)
    bpair = jnp.stack([b_spatial[:, 0::2], b_spatial[:, 1::2]], axis=-1)
    bsp = jnp.repeat(bpair, A_HEAD, axis=-1)
    conv_wp = jnp.broadcast_to(conv_w[:, :, None, :], (depth, CONV_WIDTH, SUBLANES, D_C))
    pad_g = jnp.zeros((depth, EXPERT_ROW0 - N_GROUPS, D_MODEL), F32)
    pad_e = jnp.zeros((depth, ROUTE_ROWS - EXPERT_ROW0 - N_EXPERTS, D_MODEL), F32)
    wr = jnp.concatenate([jnp.swapaxes(w_router_group, 1, 2), pad_g, jnp.swapaxes(w_router_expert, 1, 2), pad_e],
                         axis=1)
    br = jnp.concatenate([b_router_group, pad_g[:, :, 0], b_router_expert, pad_e[:, :, 0]], axis=1)[:, :, None]
    wr_hi = wr.astype(BF16)
    wr_lo = (wr - wr_hi.astype(F32)).astype(BF16)
    return dict(
        norm_mix=row(norm_mix), w_in=w_in.astype(BF16), gmlp_ln_g=row(gmlp_ln_g), gmlp_ln_b=row(gmlp_ln_b),
        wsp=wsp, bsp=bsp, conv_w=conv_wp, conv_b=row(conv_b), conv_ln_g=row(conv_ln_g), conv_ln_b=row(conv_ln_b),
        w_out_a=w_out_a.astype(BF16), w_out_b=w_out_b.astype(BF16), w_out_c=w_out_c.astype(BF16),
        w_o=w_o.astype(BF16), norm_ffn=row(norm_ffn), wr_hi=wr_hi, wr_lo=wr_lo, br=br,
        w1=w1, w3=w3, w2=w2,
        norm_ple=row(norm_ple), w_ple_gate=w_ple_gate.astype(BF16), w_ple_proj=w_ple_proj.astype(BF16),
        norm_final=norm_final.reshape(1, D_MODEL))


def kernel(x_prompt, x_sample, p_prompt, p_sample, norm_mix, w_in, gmlp_ln_g, gmlp_ln_b, w_spatial, b_spatial, conv_w, conv_b, conv_ln_g, conv_ln_b, w_out_a, w_out_b, w_out_c, w_o, norm_ffn, w_router_group, b_router_group, w_router_expert, b_router_expert, w1, w3, w2, norm_ple, w_ple_gate, w_ple_proj, norm_final):
    wts = _prepare_weights(norm_mix, w_in, gmlp_ln_g, gmlp_ln_b, w_spatial, b_spatial, conv_w, conv_b, conv_ln_g,
                           conv_ln_b, w_out_a, w_out_b, w_out_c, w_o, norm_ffn, w_router_group, b_router_group,
                           w_router_expert, b_router_expert, w1, w3, w2, norm_ple, w_ple_gate, w_ple_proj,
                           norm_final)
    return (_run_group(x_prompt, p_prompt, wts), _run_group(x_sample, p_sample, wts))
```

```python
import functools
import math

import numpy as np
import jax
import jax.numpy as jnp
from jax import lax
from jax.experimental import pallas as pl
from jax.experimental.pallas import tpu as pltpu

D_MODEL = 1024
CHUNK = 128
A_GROUPS = 8
A_HEAD = 64
D_A = A_GROUPS * A_HEAD
B_GROUPS = 4
B_HEAD = 128
D_B = B_GROUPS * B_HEAD
D_C = 512
CONV_WIDTH = 31
CONV_HALO = 16
N_BRANCH = 3
O1 = 2 * D_A
O2 = O1 + D_B
O3 = O2 + 2 * D_C
IN_WIDTH = O3 + N_BRANCH * D_MODEL
N_GROUPS = 4
E_PER_GROUP = 8
N_EXPERTS = N_GROUPS * E_PER_GROUP
D_EXPERT = 512
MOE_BLOCK = 256
PLE_DIM = 256
EPS = 1e-6

LANES = 128
SUBLANES = 8
FFT_N2 = 128
ROUTE_ROWS = 128
EXPERT_ROW0 = 8
RANK_TILE = 2048
VMEM_LIMIT = 56 * 1024 * 1024

F32 = jnp.float32
BF16 = jnp.bfloat16


def _cparams(*sem):
    return pltpu.CompilerParams(dimension_semantics=sem, vmem_limit_bytes=VMEM_LIMIT)


def _resident(shape, index=None):
    idx = tuple(index) if index is not None else (0,) * len(shape)
    return pl.BlockSpec(shape, lambda *_: idx, pipeline_mode=pl.Buffered(1))


def _rms(x, g):
    return x * lax.rsqrt(jnp.mean(x * x, axis=-1, keepdims=True) + EPS) * g


def _ln(x, g, b):
    mu = jnp.mean(x, axis=-1, keepdims=True)
    xc = x - mu
    return xc * lax.rsqrt(jnp.mean(xc * xc, axis=-1, keepdims=True) + EPS) * g + b


def _sigmoid(x):
    return 1.0 / (1.0 + jnp.exp(-x))


def _dot(a, b):
    return jnp.dot(a, b, preferred_element_type=F32)


def _pack_pair(a, b):
    hi = lax.bitcast_convert_type(a.astype(BF16).astype(F32), jnp.uint32)
    lo = lax.bitcast_convert_type(b.astype(BF16).astype(F32), jnp.uint32)
    return hi | (lo >> 16)


def _unpack_pair(p):
    a = lax.bitcast_convert_type(p & jnp.uint32(0xFFFF0000), F32)
    b = lax.bitcast_convert_type(p << 16, F32)
    return a, b


def _pack_row_halves(x):
    n = x.shape[1] // 2
    return _pack_pair(x[:, :n], x[:, n:])


def _unpack_row_halves(p):
    return jnp.concatenate(_unpack_pair(p), axis=1)


def _mix_in_compute(x, nm_ref, w_ref, lng_ref, lnb_ref, wsp_ref, bsp_ref, dft_ref,
                    ya_ref, xri_ref, yglu_ref, gates_ref, tm):
    hb = _rms(x, nm_ref[...]).astype(BF16)

    def proj(lo, hi):
        return _dot(hb, w_ref[:, lo:hi])

    za = proj(0, O1)
    za = 0.5 * za * (1.0 + lax.erf(za * (1.0 / math.sqrt(2.0))))
    u = za[:, :D_A]
    v = _ln(za[:, D_A:], lng_ref[...], lnb_ref[...])
    n_chunks = tm // CHUNK
    low_half = lax.broadcasted_iota(jnp.int32, (CHUNK, LANES), 1) < A_HEAD
    for j in range(A_GROUPS // 2):
        cols = slice(j * LANES, (j + 1) * LANES)
        pieces = []
        for c in range(n_chunks):
            vp = v[c * CHUNK:(c + 1) * CHUNK, cols]
            pieces.append(jnp.concatenate([jnp.where(low_half, vp, 0.0), jnp.where(low_half, 0.0, vp)], axis=0))
        rhs = jnp.concatenate(pieces, axis=1).astype(BF16)
        sv = _dot(wsp_ref[j], rhs)
        for c in range(n_chunks):
            rows = slice(c * CHUNK, (c + 1) * CHUNK)
            svc = sv[:, c * LANES:(c + 1) * LANES] + bsp_ref[j]
            ya_ref[rows, cols] = (u[rows, cols] * svc).astype(ya_ref.dtype)

    zb = proj(O1, O2).astype(BF16)
    for g in range(B_GROUPS):
        cols = slice(g * B_HEAD, (g + 1) * B_HEAD)
        c = _dot(zb[:, cols], dft_ref[...])
        xri_ref[:, cols] = _pack_pair(c[:, :B_HEAD], c[:, B_HEAD:])

    zc = proj(O2, O3)
    yglu_ref[...] = zc[:, :D_C] * _sigmoid(zc[:, D_C:])

    for b in range(N_BRANCH):
        cols = slice(b * D_MODEL, (b + 1) * D_MODEL)
        gates_ref[:, cols] = _sigmoid(proj(O3 + b * D_MODEL, O3 + (b + 1) * D_MODEL)).astype(gates_ref.dtype)


def _mix_in_kernel(x_ref, nm_ref, w_ref, lng_ref, lnb_ref, wsp_ref, bsp_ref, dft_ref,
                   ya_ref, xri_ref, yglu_ref, gates_ref, *, tm):
    _mix_in_compute(x_ref[...], nm_ref, w_ref, lng_ref, lnb_ref, wsp_ref, bsp_ref, dft_ref,
                    ya_ref, xri_ref, yglu_ref, gates_ref, tm)


def _layer_spec(layer, shape):
    return pl.BlockSpec((None,) + shape, lambda i: (layer,) + (0,) * len(shape), pipeline_mode=pl.Buffered(1))


def _mix_in_weight_specs(layer):
    return [_layer_spec(layer, (1, D_MODEL)), _layer_spec(layer, (D_MODEL, IN_WIDTH)), _layer_spec(layer, (1, D_A)),
            _layer_spec(layer, (1, D_A)), _layer_spec(layer, (A_GROUPS // 2, CHUNK, 2 * CHUNK)),
            _layer_spec(layer, (A_GROUPS // 2, CHUNK, LANES)), _resident((B_HEAD, 2 * B_HEAD))]


def _mix_in_out(t, tm):
    def row(w):
        return pl.BlockSpec((tm, w), lambda i: (i, 0))
    specs = [row(D_A), row(D_B), row(D_C), row(N_BRANCH * D_MODEL)]
    shapes = [jax.ShapeDtypeStruct((t, D_A), BF16), jax.ShapeDtypeStruct((t, D_B), jnp.uint32),
              jax.ShapeDtypeStruct((t, D_C), F32), jax.ShapeDtypeStruct((t, N_BRANCH * D_MODEL), BF16)]
    return specs, shapes


def _mix_in(layer, x, nm, w_in, lng, lnb, wsp, bsp, dftc, tm):
    t = x.shape[0]
    out_specs, out_shape = _mix_in_out(t, tm)
    return pl.pallas_call(
        functools.partial(_mix_in_kernel, tm=tm),
        grid=(t // tm,),
        in_specs=[pl.BlockSpec((tm, D_MODEL), lambda i: (i, 0))] + _mix_in_weight_specs(layer),
        out_specs=out_specs,
        out_shape=out_shape,
        compiler_params=_cparams("parallel"),
        name="mix_in",
    )(x, nm, w_in, lng, lnb, wsp, bsp, dftc)


FFT_SLABS = SUBLANES


def _packed_rows(p):
    re, im = _unpack_pair(p)
    return jnp.concatenate([re, im], axis=0).astype(BF16)


def _fft1_kernel(x_hbm, a_ref, twr_ref, twi_ref, y_hbm, inbuf, outbuf, sem_in, sem_out, *, n1, nj, n_steps):
    b = pl.program_id(0)
    jb = pl.program_id(1)
    step = b * nj + jb
    slot = step % 2

    def in_copy(bb, jj, j, s):
        return pltpu.make_async_copy(x_hbm.at[bb, :, jj * FFT_SLABS + j, :], inbuf.at[s, j], sem_in.at[s])

    def out_copy(bb, jj, j, s):
        return pltpu.make_async_copy(outbuf.at[s, j], y_hbm.at[bb, :, jj * FFT_SLABS + j, :], sem_out.at[s])

    @pl.when(step == 0)
    def _():
        for j in range(FFT_SLABS):
            in_copy(b, jb, j, 0).start()

    @pl.when(step + 1 < n_steps)
    def _():
        nxt = step + 1
        for j in range(FFT_SLABS):
            in_copy(nxt // nj, nxt % nj, j, 1 - slot).start()

    @pl.when(step >= 2)
    def _():
        for j in range(FFT_SLABS):
            out_copy(b, jb, j, slot).wait()

    for j in range(FFT_SLABS):
        in_copy(b, jb, j, slot).wait()
    for j in range(FFT_SLABS):
        y = _dot(a_ref[...], _packed_rows(inbuf[slot, j]))
        yr, yi = y[:n1], y[n1:]
        tr, ti = twr_ref[j], twi_ref[j]
        outbuf[slot, j] = _pack_pair(yr * tr - yi * ti, yr * ti + yi * tr)
        out_copy(b, jb, j, slot).start()

    @pl.when(step == n_steps - 1)
    def _():
        for j in range(FFT_SLABS):
            out_copy(b, jb, j, slot).wait()
        for j in range(FFT_SLABS):
            out_copy(b, jb, j, 1 - slot).wait()


def _fft1(xri, a_mat, twr, twi, bsz, n1):
    n2 = FFT_N2
    nj = n2 // FFT_SLABS
    x4 = xri.reshape(bsz, n1, n2, D_B)
    tw = pl.BlockSpec((FFT_SLABS, n1, 1), lambda b, j: (j, 0, 0))
    return pl.pallas_call(
        functools.partial(_fft1_kernel, n1=n1, nj=nj, n_steps=bsz * nj),
        grid=(bsz, nj),
        in_specs=[pl.BlockSpec(memory_space=pl.ANY), _resident((2 * n1, 2 * n1)), tw, tw],
        out_specs=pl.BlockSpec(memory_space=pl.ANY),
        out_shape=jax.ShapeDtypeStruct((bsz, n1, n2, D_B), jnp.uint32),
        scratch_shapes=[pltpu.VMEM((2, FFT_SLABS, n1, D_B), jnp.uint32), pltpu.VMEM((2, FFT_SLABS, n1, D_B), jnp.uint32),
                        pltpu.SemaphoreType.DMA((2,)), pltpu.SemaphoreType.DMA((2,))],
        compiler_params=_cparams("arbitrary", "arbitrary"),
        name="fft1",
    )(x4, a_mat, twr, twi)


def _fft2_kernel(y_ref, w_ref, o_hbm, outbuf, sem, *, nk, n_steps):
    b = pl.program_id(0)
    kb = pl.program_id(1)
    step = b * nk + kb
    slot = step % 2

    def out_copy(kk, s):
        return pltpu.make_async_copy(outbuf.at[s, kk], o_hbm.at[b, :, kb * FFT_SLABS + kk, :], sem.at[s])

    @pl.when(step >= 2)
    def _():
        for kk in range(FFT_SLABS):
            out_copy(kk, slot).wait()

    for kk in range(FFT_SLABS):
        outbuf[slot, kk] = _dot(w_ref[...], _packed_rows(y_ref[kk]))
        out_copy(kk, slot).start()

    @pl.when(step == n_steps - 1)
    def _():
        for kk in range(FFT_SLABS):
            out_copy(kk, slot).wait()
        for kk in range(FFT_SLABS):
            out_copy(kk, 1 - slot).wait()


def _fft2(ypk, w2, bsz, n1):
    n2 = FFT_N2
    nk = n1 // FFT_SLABS
    return pl.pallas_call(
        functools.partial(_fft2_kernel, nk=nk, n_steps=bsz * nk),
        grid=(bsz, nk),
        in_specs=[pl.BlockSpec((None, FFT_SLABS, n2, D_B), lambda b, k: (b, k, 0, 0)), _resident((n2, 2 * n2))],
        out_specs=pl.BlockSpec(memory_space=pl.ANY),
        out_shape=jax.ShapeDtypeStruct((bsz, n2, n1, D_B), F32),
        scratch_shapes=[pltpu.VMEM((2, FFT_SLABS, n2, D_B), F32), pltpu.SemaphoreType.DMA((2,))],
        compiler_params=_cparams("arbitrary", "arbitrary"),
        name="fft2",
    )(ypk, w2)


def _dft_constants(n1):
    n2 = FFT_N2
    k = np.arange(B_HEAD)
    ang = 2.0 * np.pi * np.outer(k, k) / B_HEAD
    dftc = np.concatenate([np.cos(ang), -np.sin(ang)], axis=1) / math.sqrt(B_HEAD)
    k1 = np.arange(n1)
    ang1 = 2.0 * np.pi * np.outer(k1, k1) / n1
    ar, ai = np.cos(ang1) / math.sqrt(n1), -np.sin(ang1) / math.sqrt(n1)
    a_mat = np.block([[ar, -ai], [ai, ar]])
    n2i = np.arange(n2)
    angt = 2.0 * np.pi * np.outer(n2i, k1) / (n1 * n2)
    twr, twi = np.cos(angt)[:, :, None], -np.sin(angt)[:, :, None]
    ang2 = 2.0 * np.pi * np.outer(n2i, n2i) / n2
    w2 = np.concatenate([np.cos(ang2), np.sin(ang2)], axis=1) / math.sqrt(n2)
    return (jnp.asarray(dftc, BF16), jnp.asarray(a_mat, BF16), jnp.asarray(twr, F32), jnp.asarray(twi, F32),
            jnp.asarray(w2, BF16))


CONV_ROWS = 32


def _conv_compute(i, cur_ref, prev_ref, next_ref, w_ref, b_ref, g_ref, bb_ref, o_ref, sh_ref, tc, seq):
    first = (i * tc) % seq == 0
    last = ((i + 1) * tc) % seq == 0
    n_ext = tc + 2 * CONV_HALO
    ext = jnp.concatenate([jnp.where(first, 0.0, prev_ref[...]), cur_ref[...],
                           jnp.where(last, 0.0, next_ref[...])], axis=0)
    sh_ref[0] = ext
    for s in range(1, SUBLANES):
        sh_ref[s] = pltpu.roll(ext, n_ext - s, axis=0)
    base = CONV_HALO - CONV_WIDTH // 2
    groups = CONV_ROWS // SUBLANES
    for r in range(tc // CONV_ROWS):
        acc = [jnp.zeros((SUBLANES, D_C), F32) for _ in range(groups)]
        for k in range(CONV_WIDTH):
            s = (base + k) % SUBLANES
            lo = r * CONV_ROWS + base + k - s
            wk = w_ref[k]
            for q in range(groups):
                acc[q] = acc[q] + wk * sh_ref[s, lo + q * SUBLANES:lo + (q + 1) * SUBLANES, :]
        y = _ln(jnp.concatenate(acc, axis=0) + b_ref[...], g_ref[...], bb_ref[...])
        o_ref[r * CONV_ROWS:(r + 1) * CONV_ROWS] = (y * _sigmoid(y)).astype(o_ref.dtype)


def _mix_out_kernel(x_ref, ya_ref, yb_ref, glu_ref, glu_prev_ref, glu_next_ref, gt_ref,
                    cw_ref, cb_ref, cg_ref, cbb_ref, wa_ref, wb_ref, wc_ref, wo_ref, nf_ref,
                    wrh_ref, wrl_ref, br_ref, x1_ref, hn_ref, ri_ref, rw_ref, yc_ref, sh_ref, *, tm, seq):
    _conv_compute(pl.program_id(0), glu_ref, glu_prev_ref, glu_next_ref, cw_ref, cb_ref, cg_ref, cbb_ref,
                  yc_ref, sh_ref, tm, seq)
    m = gt_ref[:, 0:D_MODEL].astype(F32) * _dot(ya_ref[...], wa_ref[...])
    m = m + gt_ref[:, D_MODEL:2 * D_MODEL].astype(F32) * _dot(yb_ref[...].astype(BF16), wb_ref[...])
    m = m + gt_ref[:, 2 * D_MODEL:3 * D_MODEL].astype(F32) * _dot(yc_ref[...], wc_ref[...])
    x1 = x_ref[...] + _dot(m.astype(BF16), wo_ref[...])
    x1_ref[...] = x1
    hn = _rms(x1, nf_ref[...])
    hn_ref[...] = _pack_row_halves(hn)

    hn_hi = hn.astype(BF16)
    hn_lo = (hn - hn_hi.astype(F32)).astype(BF16)
    nt = (((1,), (1,)), ((), ()))
    lg = (lax.dot_general(wrh_ref[...], hn_hi, nt, preferred_element_type=F32)
          + lax.dot_general(wrh_ref[...], hn_lo, nt, preferred_element_type=F32)
          + lax.dot_general(wrl_ref[...], hn_hi, nt, preferred_element_type=F32)) + br_ref[...]
    gl = [lg[g:g + 1, :] for g in range(N_GROUPS)]
    best = gl[0]
    gsel = jnp.zeros((1, tm), jnp.int32)
    for g in range(1, N_GROUPS):
        better = gl[g] > best
        best = jnp.where(better, gl[g], best)
        gsel = jnp.where(better, g, gsel)
    denom = jnp.exp(gl[0] - best)
    for g in range(1, N_GROUPS):
        denom = denom + jnp.exp(gl[g] - best)
    pg = 1.0 / denom
    es = lg[EXPERT_ROW0:EXPERT_ROW0 + E_PER_GROUP, :]
    for g in range(1, N_GROUPS):
        lo = EXPERT_ROW0 + g * E_PER_GROUP
        es = jnp.where(gsel == g, lg[lo:lo + E_PER_GROUP, :], es)
    slot = lax.broadcasted_iota(jnp.int32, (E_PER_GROUP, tm), 0)
    v0 = jnp.max(es, axis=0, keepdims=True)
    i0 = jnp.min(jnp.where(es == v0, slot, E_PER_GROUP), axis=0, keepdims=True)
    es1 = jnp.where(slot == i0, -jnp.inf, es)
    v1 = jnp.max(es1, axis=0, keepdims=True)
    i1 = jnp.min(jnp.where(es1 == v1, slot, E_PER_GROUP), axis=0, keepdims=True)
    ex = jnp.exp(v1 - v0)
    p0 = 1.0 / (1.0 + ex)
    p1 = ex / (1.0 + ex)
    row = lax.broadcasted_iota(jnp.int32, (SUBLANES, tm), 0)
    e0 = gsel * E_PER_GROUP + i0
    e1 = gsel * E_PER_GROUP + i1
    ri_ref[...] = jnp.where(row == 0, e0, jnp.where(row == 1, e1, 0))
    wrow = lax.broadcasted_iota(jnp.int32, (LANES, tm), 0)
    rw_ref[...] = jnp.where(wrow == 0, pg * p0, jnp.where(wrow == 1, pg * p1, 0.0)).T


def _mix_out(layer, x, ya, yb, yglu, gates, conv_w, conv_b, conv_lng, conv_lnb, wa, wb, wc, wo, nf,
             wr_hi, wr_lo, br, seq, tm):
    t = x.shape[0]
    hb = tm // CONV_HALO
    n_halo = t // CONV_HALO

    def row(w):
        return pl.BlockSpec((tm, w), lambda i: (i, 0))

    def lay(shape):
        return _layer_spec(layer, shape)

    return pl.pallas_call(
        functools.partial(_mix_out_kernel, tm=tm, seq=seq),
        grid=(t // tm,),
        in_specs=[row(D_MODEL), row(D_A), row(D_B), row(D_C),
                  pl.BlockSpec((CONV_HALO, D_C), lambda i: (jnp.maximum(i * hb - 1, 0), 0)),
                  pl.BlockSpec((CONV_HALO, D_C), lambda i: (jnp.minimum((i + 1) * hb, n_halo - 1), 0)),
                  row(N_BRANCH * D_MODEL),
                  lay((CONV_WIDTH, SUBLANES, D_C)), lay((1, D_C)), lay((1, D_C)), lay((1, D_C)),
                  lay((D_A, D_MODEL)), lay((D_B, D_MODEL)), lay((D_C, D_MODEL)), lay((D_MODEL, D_MODEL)),
                  lay((1, D_MODEL)), lay((ROUTE_ROWS, D_MODEL)), lay((ROUTE_ROWS, D_MODEL)), lay((ROUTE_ROWS, 1))],
        out_specs=[row(D_MODEL), row(D_MODEL // 2), pl.BlockSpec((SUBLANES, tm), lambda i: (0, i)), row(LANES)],
        out_shape=[jax.ShapeDtypeStruct((t, D_MODEL), F32), jax.ShapeDtypeStruct((t, D_MODEL // 2), jnp.uint32),
                   jax.ShapeDtypeStruct((SUBLANES, t), jnp.int32), jax.ShapeDtypeStruct((t, LANES), F32)],
        scratch_shapes=[pltpu.VMEM((tm, D_C), BF16), pltpu.VMEM((SUBLANES, tm + 2 * CONV_HALO, D_C), F32)],
        compiler_params=_cparams("parallel"),
        name="mix_out",
    )(x, ya, yb, yglu, yglu, yglu, gates, conv_w, conv_b, conv_lng, conv_lnb, wa, wb, wc, wo, nf, wr_hi, wr_lo, br)


def _rank_kernel(e_ref, u_ref, rank_ref, cnt_ref, run_ref):
    k = pl.program_id(0)
    j = pl.program_id(1)

    @pl.when((k == 0) & (j == 0))
    def _():
        run_ref[...] = jnp.zeros_like(run_ref)

    tl = e_ref.shape[1]
    e = jnp.where(k == 0, e_ref[0:1, :], e_ref[1:2, :])
    onehot = lax.broadcasted_iota(jnp.int32, (N_EXPERTS, tl), 0) == e
    csum = _dot(jnp.where(onehot, 1.0, 0.0).astype(BF16), u_ref[...])
    base = run_ref[:, 0:1]
    rank = jnp.sum(jnp.where(onehot, csum - 1.0 + base, 0.0), axis=0, keepdims=True)
    rank_ref[...] = rank.astype(jnp.int32)
    run_ref[...] = run_ref[...] + csum[:, tl - 1:tl]
    cnt_ref[...] = run_ref[...]


def _route_plan(ri, t):
    tl = RANK_TILE if t % RANK_TILE == 0 else CHUNK
    upper = jnp.asarray(np.triu(np.ones((tl, tl), np.float32)), BF16)
    rank, cnt = pl.pallas_call(
        _rank_kernel,
        grid=(2, t // tl),
        in_specs=[pl.BlockSpec((SUBLANES, tl), lambda k, j: (0, j)), _resident((tl, tl))],
        out_specs=[pl.BlockSpec((None, 1, tl), lambda k, j: (k, 0, j)), _resident((N_EXPERTS, LANES))],
        out_shape=[jax.ShapeDtypeStruct((2, 1, t), jnp.int32), jax.ShapeDtypeStruct((N_EXPERTS, LANES), F32)],
        scratch_shapes=[pltpu.VMEM((N_EXPERTS, LANES), F32)],
        compiler_params=_cparams("arbitrary", "arbitrary"),
        name="rank",
    )(ri, upper)
    counts = cnt[:, 0].astype(jnp.int32)
    padded = (counts + MOE_BLOCK - 1) // MOE_BLOCK * MOE_BLOCK
    pad_end = jnp.cumsum(padded)
    pad_start = pad_end - padded
    experts = jnp.arange(N_EXPERTS, dtype=jnp.int32)
    dest = rank[:, 0, :] + jnp.sum(jnp.where(ri[:2, :, None] == experts, pad_start, 0), axis=-1)
    n_blocks = 2 * t // MOE_BLOCK + N_EXPERTS
    first_row = jnp.arange(n_blocks, dtype=jnp.int32) * MOE_BLOCK
    block_e = jnp.minimum(jnp.sum((pad_end[None, :] <= first_row[:, None]).astype(jnp.int32), axis=1), N_EXPERTS - 1)
    n_used = (pad_end[-1] // MOE_BLOCK).astype(jnp.int32).reshape(1)
    tail = pad_end[-1] + experts * MOE_BLOCK
    zero_blk = jnp.concatenate([jnp.where(padded > 0, pad_end - MOE_BLOCK, -1),
                                jnp.where(tail < n_blocks * MOE_BLOCK, tail, -1)]).astype(jnp.int32)
    return dest.astype(jnp.int32), block_e.astype(jnp.int32), n_used, zero_blk, n_blocks


DISPATCH_UNROLL = 8


def _dispatch_kernel(zb_ref, d_ref, hn_ref, xg_hbm, zero_ref, sem, *, td):
    i = pl.program_id(0)

    def zero_copy(e):
        start = pl.multiple_of(zb_ref[e], MOE_BLOCK)
        return pltpu.make_async_copy(zero_ref, xg_hbm.at[pl.ds(start, MOE_BLOCK)], sem.at[1])

    @pl.when(i == 0)
    def _():
        zero_ref[...] = jnp.zeros_like(zero_ref)
        for e in range(2 * N_EXPERTS):
            @pl.when(zb_ref[e] >= 0)
            def _():
                zero_copy(e).start()
        for e in range(2 * N_EXPERTS):
            @pl.when(zb_ref[e] >= 0)
            def _():
                zero_copy(e).wait()

    def body(c, carry):
        for u in range(DISPATCH_UNROLL):
            r = c * DISPATCH_UNROLL + u
            src = hn_ref.at[pl.ds(r, 1)]
            pltpu.make_async_copy(src, xg_hbm.at[pl.ds(d_ref[0, 0, r], 1)], sem.at[0]).start()
            pltpu.make_async_copy(src, xg_hbm.at[pl.ds(d_ref[0, 0, td + r], 1)], sem.at[0]).start()
        return carry

    lax.fori_loop(0, td // DISPATCH_UNROLL, body, 0)
    for _ in range(2):
        pltpu.make_async_copy(hn_ref, xg_hbm.at[pl.ds(0, td)], sem.at[0]).wait()


def _dest_tiles(dest, n_tiles, tile):
    return dest.reshape(2, n_tiles, tile).transpose(1, 0, 2).reshape(n_tiles, 1, 2 * tile)


def _dispatch(hn, dest, zero_blk, n_blocks, td):
    t = hn.shape[0]
    n_tiles = t // td
    grid_spec = pltpu.PrefetchScalarGridSpec(
        num_scalar_prefetch=1,
        grid=(n_tiles,),
        in_specs=[pl.BlockSpec((1, 1, 2 * td), lambda i, zb: (i, 0, 0), memory_space=pltpu.SMEM),
                  pl.BlockSpec((td, D_MODEL // 2), lambda i, zb: (i, 0))],
        out_specs=pl.BlockSpec(memory_space=pl.ANY),
        scratch_shapes=[pltpu.VMEM((MOE_BLOCK, D_MODEL // 2), jnp.uint32), pltpu.SemaphoreType.DMA((2,))],
    )
    return pl.pallas_call(
        functools.partial(_dispatch_kernel, td=td),
        grid_spec=grid_spec,
        out_shape=jax.ShapeDtypeStruct((n_blocks * MOE_BLOCK, D_MODEL // 2), jnp.uint32),
        compiler_params=_cparams("arbitrary"),
        name="dispatch",
    )(zero_blk, _dest_tiles(dest, n_tiles, td), hn)


def _expert_kernel(be_ref, nu_ref, x_ref, w1_ref, w3_ref, w2_ref, o_ref, w1b, w3b, w2b):
    i = pl.program_id(0)

    @pl.when((i == 0) | (be_ref[i] != be_ref[jnp.maximum(i - 1, 0)]))
    def _():
        w1b[...] = w1_ref[...].astype(BF16)
        w3b[...] = w3_ref[...].astype(BF16)
        w2b[...] = w2_ref[...].astype(BF16)

    @pl.when(i < nu_ref[0])
    def _():
        xb = _unpack_row_halves(x_ref[...]).astype(BF16)
        h1 = _dot(xb, w1b[...])
        h3 = _dot(xb, w3b[...])
        act = (h1 * _sigmoid(h1) * h3).astype(BF16)
        o_ref[...] = _pack_row_halves(_dot(act, w2b[...]))

    @pl.when(i >= nu_ref[0])
    def _():
        o_ref[...] = jnp.zeros_like(o_ref)


def _experts(layer, xg, block_e, n_used, n_blocks, w1, w3, w2):
    grid_spec = pltpu.PrefetchScalarGridSpec(
        num_scalar_prefetch=2,
        grid=(n_blocks,),
        in_specs=[
            pl.BlockSpec((MOE_BLOCK, D_MODEL // 2), lambda i, be, nu: (jnp.minimum(i, nu[0] - 1), 0)),
            pl.BlockSpec((None, None, D_MODEL, D_EXPERT), lambda i, be, nu: (layer, be[i], 0, 0)),
            pl.BlockSpec((None, None, D_MODEL, D_EXPERT), lambda i, be, nu: (layer, be[i], 0, 0)),
            pl.BlockSpec((None, None, D_EXPERT, D_MODEL), lambda i, be, nu: (layer, be[i], 0, 0)),
        ],
        out_specs=pl.BlockSpec((MOE_BLOCK, D_MODEL // 2), lambda i, be, nu: (i, 0)),
        scratch_shapes=[pltpu.VMEM((D_MODEL, D_EXPERT), BF16), pltpu.VMEM((D_MODEL, D_EXPERT), BF16),
                        pltpu.VMEM((D_EXPERT, D_MODEL), BF16)],
    )
    return pl.pallas_call(
        _expert_kernel,
        grid_spec=grid_spec,
        out_shape=jax.ShapeDtypeStruct((n_blocks * MOE_BLOCK, D_MODEL // 2), jnp.uint32),
        compiler_params=_cparams("arbitrary"),
        name="experts",
    )(block_e, n_used, xg, w1, w3, w2)


def _gather_rows(yg_hbm, dref, buf, sem, n_rows, unrolled):
    def start(r):
        pltpu.make_async_copy(yg_hbm.at[pl.ds(dref[0, 0, r], 1)], buf.at[pl.ds(r, 1)], sem).start()

    if unrolled:
        for r in range(n_rows):
            start(r)
        return

    def body(c, carry):
        for u in range(DISPATCH_UNROLL):
            start(c * DISPATCH_UNROLL + u)
        return carry
    lax.fori_loop(0, n_rows // DISPATCH_UNROLL, body, 0)


def _combine_compute(buf, x1_ref, rw_ref, p_ref, np_ref, wg_ref, wp_ref, tq):
    moe = (rw_ref[:, 0:1] * _unpack_row_halves(buf[0:tq, :])
           + rw_ref[:, 1:2] * _unpack_row_halves(buf[tq:2 * tq, :]))
    x2 = x1_ref[...] + moe
    g = _sigmoid(_dot(_rms(x2, np_ref[...]).astype(BF16), wg_ref[...]))
    return x2 + g * _dot(p_ref[...].astype(BF16), wp_ref[...])


def _combine_steps(i, n_tiles, tq, d_ref, dn_ref, yg_hbm, bufs, sem, unrolled, compute):
    def wait(s):
        pltpu.make_async_copy(yg_hbm.at[pl.ds(0, 2 * tq)], bufs[s], sem.at[s]).wait()

    @pl.when(i == 0)
    def _():
        _gather_rows(yg_hbm, d_ref, bufs[0], sem.at[0], 2 * tq, False)

    for parity in range(2):
        @pl.when(i % 2 == parity)
        def _():
            wait(parity)
            _gather_rows(yg_hbm, dn_ref, bufs[1 - parity], sem.at[1 - parity], 2 * tq, unrolled)
            compute(bufs[parity])

            @pl.when(i == n_tiles - 1)
            def _():
                wait(1 - parity)


def _combine_kernel(d_ref, dn_ref, yg_hbm, x1_ref, rw_ref, p_ref, np_ref, wg_ref, wp_ref, nfin_ref, o_ref,
                    buf0, buf1, sem, *, tq, n_tiles, final):
    def compute(buf):
        x3 = _combine_compute(buf, x1_ref, rw_ref, p_ref, np_ref, wg_ref, wp_ref, tq)
        o_ref[...] = _rms(x3, nfin_ref[...]) if final else x3

    _combine_steps(pl.program_id(0), n_tiles, tq, d_ref, dn_ref, yg_hbm, (buf0, buf1), sem, False, compute)


def _combine_mix_in_kernel(d_ref, dn_ref, yg_hbm, x1_ref, rw_ref, p_ref, np_ref, wg_ref, wp_ref,
                           nm_ref, w_ref, lng_ref, lnb_ref, wsp_ref, bsp_ref, dft_ref,
                           o_ref, ya_ref, xri_ref, yglu_ref, gates_ref, buf0, buf1, sem, *, tq, n_tiles):
    def compute(buf):
        x3 = _combine_compute(buf, x1_ref, rw_ref, p_ref, np_ref, wg_ref, wp_ref, tq)
        o_ref[...] = x3
        _mix_in_compute(x3, nm_ref, w_ref, lng_ref, lnb_ref, wsp_ref, bsp_ref, dft_ref,
                        ya_ref, xri_ref, yglu_ref, gates_ref, tq)

    _combine_steps(pl.program_id(0), n_tiles, tq, d_ref, dn_ref, yg_hbm, (buf0, buf1), sem, True, compute)


def _combine_in_specs(layer, tq, n_tiles):
    return [
        pl.BlockSpec((1, 1, 2 * tq), lambda i: (i, 0, 0), memory_space=pltpu.SMEM),
        pl.BlockSpec((1, 1, 2 * tq), lambda i: (jnp.minimum(i + 1, n_tiles - 1), 0, 0), memory_space=pltpu.SMEM),
        pl.BlockSpec(memory_space=pl.ANY),
        pl.BlockSpec((tq, D_MODEL), lambda i: (i, 0)),
        pl.BlockSpec((tq, LANES), lambda i: (i, 0)),
        pl.BlockSpec((None, tq, PLE_DIM), lambda i: (layer, i, 0)),
        _layer_spec(layer, (1, D_MODEL)), _layer_spec(layer, (D_MODEL, D_MODEL)), _layer_spec(layer, (PLE_DIM, D_MODEL)),
    ]


def _combine_scratch(tq):
    buf = pltpu.VMEM((2 * tq, D_MODEL // 2), jnp.uint32)
    return [buf, buf, pltpu.SemaphoreType.DMA((2,))]


def _combine(layer, dest, yg, x1, rw, p, norm_ple, wg, wp, nfin, tq, final):
    t = x1.shape[0]
    n_tiles = t // tq
    d3 = _dest_tiles(dest, n_tiles, tq)
    return pl.pallas_call(
        functools.partial(_combine_kernel, tq=tq, n_tiles=n_tiles, final=final),
        grid=(n_tiles,),
        in_specs=_combine_in_specs(layer, tq, n_tiles) + [_resident((1, D_MODEL))],
        out_specs=pl.BlockSpec((tq, D_MODEL), lambda i: (i, 0)),
        out_shape=jax.ShapeDtypeStruct((t, D_MODEL), F32),
        scratch_shapes=_combine_scratch(tq),
        compiler_params=_cparams("arbitrary"),
        name="combine",
    )(d3, d3, yg, x1, rw, p, norm_ple, wg, wp, nfin)


def _combine_mix_in(layer, dest, yg, x1, rw, p, norm_ple, wg, wp, mix_weights, tq):
    t = x1.shape[0]
    n_tiles = t // tq
    d3 = _dest_tiles(dest, n_tiles, tq)
    mix_specs, mix_shapes = _mix_in_out(t, tq)
    return pl.pallas_call(
        functools.partial(_combine_mix_in_kernel, tq=tq, n_tiles=n_tiles),
        grid=(n_tiles,),
        in_specs=_combine_in_specs(layer, tq, n_tiles) + _mix_in_weight_specs(layer + 1),
        out_specs=[pl.BlockSpec((tq, D_MODEL), lambda i: (i, 0))] + mix_specs,
        out_shape=[jax.ShapeDtypeStruct((t, D_MODEL), F32)] + mix_shapes,
        scratch_shapes=_combine_scratch(tq),
        compiler_params=_cparams("arbitrary"),
        name="combine_mix_in",
    )(d3, d3, yg, x1, rw, p, norm_ple, wg, wp, *mix_weights)


def _tile(t, want):
    return want if t % want == 0 else CHUNK


def _run_group(x, p, wts):
    bsz, seq, _ = x.shape
    depth = p.shape[0]
    t = bsz * seq
    n1 = seq // FFT_N2
    tm = _tile(t, 512)
    tq = _tile(t, 256)
    dftc, a_mat, twr, twi, w2f = _dft_constants(n1)
    mix_weights = (wts['norm_mix'], wts['w_in'], wts['gmlp_ln_g'], wts['gmlp_ln_b'], wts['wsp'], wts['bsp'], dftc)
    ple = (wts['norm_ple'], wts['w_ple_gate'], wts['w_ple_proj'])
    xf = x.reshape(t, D_MODEL)
    pf = p.reshape(depth, t, PLE_DIM)
    ya, xri, yglu, gates = _mix_in(0, xf, *mix_weights, tm)
    for l in range(depth):
        ypk = _fft1(xri, a_mat, twr, twi, bsz, n1)
        yb = _fft2(ypk, w2f, bsz, n1).reshape(t, D_B)
        x1, hn, ri, rw = _mix_out(l, xf, ya, yb, yglu, gates, wts['conv_w'], wts['conv_b'], wts['conv_ln_g'],
                                  wts['conv_ln_b'], wts['w_out_a'], wts['w_out_b'], wts['w_out_c'], wts['w_o'],
                                  wts['norm_ffn'], wts['wr_hi'], wts['wr_lo'], wts['br'], seq, tm)
        dest, block_e, n_used, zero_blk, n_blocks = _route_plan(ri, t)
        xg = _dispatch(hn, dest, zero_blk, n_blocks, _tile(t, 1024))
        yg = _experts(l, xg, block_e, n_used, n_blocks, wts['w1'], wts['w3'], wts['w2'])
        if l + 1 < depth:
            xf, ya, xri, yglu, gates = _combine_mix_in(l, dest, yg, x1, rw, pf, *ple, mix_weights, tq)
        else:
            xf = _combine(l, dest, yg, x1, rw, pf, *ple, wts['norm_final'], tq, True)
    return xf.reshape(bsz, seq, D_MODEL)


def _prepare_weights(norm_mix, w_in, gmlp_ln_g, gmlp_ln_b, w_spatial, b_spatial, conv_w, conv_b, conv_ln_g,
                     conv_ln_b, w_out_a, w_out_b, w_out_c, w_o, norm_ffn, w_router_group, b_router_group,
                     w_router_expert, b_router_expert, w1, w3, w2, norm_ple, w_ple_gate, w_ple_proj, norm_final):
    depth = w_in.shape[0]
    row = lambda a: a.reshape(depth, 1, -1)
    wsp = jnp.concatenate([w_spatial[:, 0::2], w_spatial[:, 1::2]], axis=-1).astype(BF16)
    bpair = jnp.stack([b_spatial[:, 0::2], b_spatial[:, 1::2]], axis=-1)
    bsp = jnp.repeat(bpair, A_HEAD, axis=-1)
    conv_wp = jnp.broadcast_to(conv_w[:, :, None, :], (depth, CONV_WIDTH, SUBLANES, D_C))
    pad_g = jnp.zeros((depth, EXPERT_ROW0 - N_GROUPS, D_MODEL), F32)
    pad_e = jnp.zeros((depth, ROUTE_ROWS - EXPERT_ROW0 - N_EXPERTS, D_MODEL), F32)
    wr = jnp.concatenate([jnp.swapaxes(w_router_group, 1, 2), pad_g, jnp.swapaxes(w_router_expert, 1, 2), pad_e],
                         axis=1)
    br = jnp.concatenate([b_router_group, pad_g[:, :, 0], b_router_expert, pad_e[:, :, 0]], axis=1)[:, :, None]
    wr_hi = wr.astype(BF16)
    wr_lo = (wr - wr_hi.astype(F32)).astype(BF16)
    return dict(
        norm_mix=row(norm_mix), w_in=w_in.astype(BF16), gmlp_ln_g=row(gmlp_ln_g), gmlp_ln_b=row(gmlp_ln_b),
        wsp=wsp, bsp=bsp, conv_w=conv_wp, conv_b=row(conv_b), conv_ln_g=row(conv_ln_g), conv_ln_b=row(conv_ln_b),
        w_out_a=w_out_a.astype(BF16), w_out_b=w_out_b.astype(BF16), w_out_c=w_out_c.astype(BF16),
        w_o=w_o.astype(BF16), norm_ffn=row(norm_ffn), wr_hi=wr_hi, wr_lo=wr_lo, br=br,
        w1=w1, w3=w3, w2=w2,
        norm_ple=row(norm_ple), w_ple_gate=w_ple_gate.astype(BF16), w_ple_proj=w_ple_proj.astype(BF16),
        norm_final=norm_final.reshape(1, D_MODEL))


def kernel(x_prompt, x_sample, p_prompt, p_sample, norm_mix, w_in, gmlp_ln_g, gmlp_ln_b, w_spatial, b_spatial, conv_w, conv_b, conv_ln_g, conv_ln_b, w_out_a, w_out_b, w_out_c, w_o, norm_ffn, w_router_group, b_router_group, w_router_expert, b_router_expert, w1, w3, w2, norm_ple, w_ple_gate, w_ple_proj, norm_final):
    wts = _prepare_weights(norm_mix, w_in, gmlp_ln_g, gmlp_ln_b, w_spatial, b_spatial, conv_w, conv_b, conv_ln_g,
                           conv_ln_b, w_out_a, w_out_b, w_out_c, w_o, norm_ffn, w_router_group, b_router_group,
                           w_router_expert, b_router_expert, w1, w3, w2, norm_ple, w_ple_gate, w_ple_proj,
                           norm_final)
    return (_run_group(x_prompt, p_prompt, wts), _run_group(x_sample, p_sample, wts))
```
